```python
import math
import jax
import jax.numpy as jnp
from jax import lax
import numpy as np

D_MODEL = 1024
BATCH = 4
SEQ = 4096
DEPTH = 4

N_MIXERS = 2
N_A_LAYERS = (DEPTH + N_MIXERS - 1) // N_MIXERS
N_B_LAYERS = DEPTH // N_MIXERS
A_HEADS = 8
A_HEAD_DIM = D_MODEL // A_HEADS
A_Q_RANK = 384
A_KV_RANK = 256
IDX_HEADS = 8
IDX_DIM = 64
TOPK_MAX = 256
Q_BLOCK = 128
A_IN_COLS = A_Q_RANK + A_KV_RANK + IDX_DIM + IDX_HEADS
B_HEADS = 8
B_KEY_DIM = D_MODEL // B_HEADS
B_VAL_DIM = D_MODEL // B_HEADS
B_CHUNK = 64
D_FF = 2816
CONV_WIDTH = 3
DN_ALPHA = (2 * DEPTH) ** 0.25
DN_BETA = (8 * DEPTH) ** -0.25
LN_EPS = 1e-5
RMS_EPS = 1e-6

kernel_name = 'dsa_hgrn2_convffn_deepnorm_hybrid'


def layer_norm(x, g, b):
    xf = x.astype(jnp.float32)
    mu = jnp.mean(xf, axis=-1, keepdims=True)
    var = jnp.mean(jnp.square(xf - mu), axis=-1, keepdims=True)
    return ((xf - mu) * lax.rsqrt(var + LN_EPS) * g + b).astype(x.dtype)


def rms_norm(x, g):
    xf = x.astype(jnp.float32)
    y = xf * lax.rsqrt(jnp.mean(jnp.square(xf), axis=-1, keepdims=True) + RMS_EPS)
    return (y * g).astype(x.dtype)


def dsa_mixer(x, w_in, g_q, g_kv, w_q_lat, w_q_idx, g_kidx, b_kidx, w_uv, w_out):
    bsz, seq, _ = x.shape
    top_k = min(TOPK_MAX, seq // 4)
    n_blk = seq // Q_BLOCK
    proj = x @ w_in
    o1 = A_Q_RANK
    o2 = o1 + A_KV_RANK
    o3 = o2 + IDX_DIM
    c_q = rms_norm(proj[..., :o1], g_q)
    c_kv = rms_norm(proj[..., o1:o2], g_kv)
    k_idx = layer_norm(proj[..., o2:o3], g_kidx, b_kidx)
    w_idx = proj[..., o3:] * (IDX_HEADS ** -0.5 * IDX_DIM ** -0.5)
    q_lat = (c_q @ w_q_lat).reshape(bsz, seq, A_HEADS, A_KV_RANK)
    q_idx = (c_q @ w_q_idx).reshape(bsz, seq, IDX_HEADS, IDX_DIM)

    def blocks(t):
        return jnp.swapaxes(t.reshape((bsz, n_blk, Q_BLOCK) + t.shape[2:]), 0, 1)

    key_pos = jnp.arange(seq, dtype=jnp.int32)
    scale = A_KV_RANK ** -0.5

    def attend_block(args):
        q_b, qi_b, wi_b, start = args
        q_pos = start + jnp.arange(Q_BLOCK, dtype=jnp.int32)
        rel = jax.nn.relu(jnp.einsum('bqhd,bsd->bqhs', qi_b, k_idx))
        score = jnp.einsum('bqh,bqhs->bqs', wi_b, rel).astype(jnp.float32)
        score = jnp.where(key_pos[None, None, :] <= q_pos[None, :, None], score, -jnp.inf)
        _, idx = lax.top_k(score, top_k)
        valid = idx <= q_pos[None, :, None]
        kv_sel = jax.vmap(lambda c, i: c[i])(c_kv, idx)
        logits = jnp.einsum('bqhr,bqkr->bhqk', q_b, kv_sel).astype(jnp.float32) * scale
        logits = jnp.where(valid[:, None], logits, -jnp.inf)
        p = jax.nn.softmax(logits, axis=-1).astype(kv_sel.dtype)
        return jnp.einsum('bhqk,bqkr->bqhr', p, kv_sel)

    starts = jnp.arange(n_blk, dtype=jnp.int32) * Q_BLOCK
    o_lat = lax.map(attend_block, (blocks(q_lat), blocks(q_idx), blocks(w_idx), starts))
    o_lat = jnp.swapaxes(o_lat, 0, 1).reshape(bsz, seq, A_HEADS, A_KV_RANK)
    o = jnp.einsum('bshr,hrd->bshd', o_lat, w_uv).reshape(bsz, seq, A_HEADS * A_HEAD_DIM)
    return o @ w_out


def hgrn2_mixer(x, w_in, lb, g_o, w_out):
    bsz, seq, _ = x.shape
    n_chunk = seq // B_CHUNK
    f32 = jnp.float32
    proj = x @ w_in
    q_raw, f_raw, i_in, g_raw = jnp.split(proj, 4, axis=-1)
    f_raw = f_raw.astype(f32)
    forget = lb + (1.0 - lb) * jax.nn.sigmoid(f_raw)
    log_f = jnp.log(forget)
    key = (1.0 - lb) * jax.nn.sigmoid(-f_raw)
    q = jax.nn.silu(q_raw.astype(f32))
    v = i_in.astype(f32)

    def chunks(t):
        return t.reshape(bsz, n_chunk, B_CHUNK, B_HEADS, -1).transpose(1, 0, 3, 2, 4)

    causal = jnp.tril(jnp.ones((B_CHUNK, B_CHUNK), dtype=bool))[:, :, None]

    def step(state, inp):
        qc, kc, vc, gc = inp
        b = jnp.cumsum(gc, axis=2)
        diff = jnp.where(causal, b[:, :, :, None, :] - b[:, :, None, :, :], -jnp.inf)
        scores = jnp.einsum('bhtd,bhsd,bhtsd->bhts', qc, kc, jnp.exp(diff))
        o = (jnp.einsum('bhts,bhse->bhte', scores, vc)
             + jnp.einsum('bhtd,bhde->bhte', qc * jnp.exp(b), state))
        b_last = b[:, :, -1:, :]
        state = (jnp.exp(b_last[:, :, 0, :, None]) * state
                 + jnp.einsum('bhsd,bhse->bhde', kc * jnp.exp(b_last - b), vc))
        return state, o

    s0 = jnp.zeros((bsz, B_HEADS, B_KEY_DIM, B_VAL_DIM), f32)
    _, o = lax.scan(step, s0, (chunks(q), chunks(key), chunks(v), chunks(log_f)))
    o = o.transpose(1, 0, 3, 2, 4).reshape(bsz, seq, B_HEADS, B_VAL_DIM)
    gate = jax.nn.sigmoid(g_raw.astype(f32)).reshape(bsz, seq, B_HEADS, B_VAL_DIM)
    o = rms_norm(o * gate, g_o.reshape(B_HEADS, B_VAL_DIM)).astype(x.dtype).reshape(bsz, seq, D_MODEL)
    return o @ w_out


def conv_ffn(x, w_up, conv_w, conv_b, w_down):
    h = x @ w_up
    h = lax.conv_general_dilated(h, conv_w.astype(h.dtype), window_strides=(1,),
                                 padding=[(CONV_WIDTH - 1, 0)],
                                 dimension_numbers=('NWC', 'WIO', 'NWC'),
                                 feature_group_count=2 * D_FF) + conv_b
    a, u = jnp.split(h, 2, axis=-1)
    return (jax.nn.silu(a) * u) @ w_down


def setup_inputs(seed: int = 0) -> dict:
    key = jax.random.key(seed)
    ks = jax.random.split(key, 22)
    f32 = jnp.float32
    nrm = lambda k, shape, s: jax.random.normal(k, shape, f32) * s
    gain = lambda k, shape: 1.0 + 0.01 * jax.random.normal(k, shape, f32)
    beta = DN_BETA
    b_col_scale = jnp.concatenate([jnp.ones((2 * D_MODEL,), f32), jnp.full((D_MODEL,), beta, f32),
                                   jnp.ones((D_MODEL,), f32)])
    return {
        'x': nrm(ks[0], (BATCH, SEQ, D_MODEL), 1.0),
        'a_w_in': nrm(ks[1], (N_A_LAYERS, D_MODEL, A_IN_COLS), D_MODEL ** -0.5),
        'a_g_q': gain(ks[2], (N_A_LAYERS, A_Q_RANK)),
        'a_g_kv': gain(ks[3], (N_A_LAYERS, A_KV_RANK)),
        'a_w_q_lat': nrm(ks[4], (N_A_LAYERS, A_Q_RANK, A_HEADS * A_KV_RANK), A_Q_RANK ** -0.5),
        'a_w_q_idx': nrm(ks[5], (N_A_LAYERS, A_Q_RANK, IDX_HEADS * IDX_DIM), A_Q_RANK ** -0.5),
        'a_g_kidx': gain(ks[6], (N_A_LAYERS, IDX_DIM)),
        'a_b_kidx': nrm(ks[7], (N_A_LAYERS, IDX_DIM), 0.01),
        'a_w_uv': nrm(ks[8], (N_A_LAYERS, A_HEADS, A_KV_RANK, A_HEAD_DIM), A_KV_RANK ** -0.5 * beta),
        'a_w_out': nrm(ks[9], (N_A_LAYERS, A_HEADS * A_HEAD_DIM, D_MODEL), (A_HEADS * A_HEAD_DIM) ** -0.5 * beta),
        'b_w_in': nrm(ks[10], (N_B_LAYERS, D_MODEL, 4 * D_MODEL), D_MODEL ** -0.5) * b_col_scale,
        'b_lb_logits': nrm(ks[11], (DEPTH, B_HEADS * B_KEY_DIM), 0.1),
        'b_g_o': gain(ks[12], (N_B_LAYERS, D_MODEL)),
        'b_w_out': nrm(ks[13], (N_B_LAYERS, D_MODEL, D_MODEL), D_MODEL ** -0.5 * beta),
        'ln1_g': gain(ks[14], (DEPTH, D_MODEL)),
        'ln1_b': nrm(ks[15], (DEPTH, D_MODEL), 0.01),
        'f_w_up': nrm(ks[16], (DEPTH, D_MODEL, 2 * D_FF), D_MODEL ** -0.5 * beta),
        'f_conv_w': nrm(ks[17], (DEPTH, CONV_WIDTH, 1, 2 * D_FF), CONV_WIDTH ** -0.5),
        'f_conv_b': nrm(ks[18], (DEPTH, 2 * D_FF), 0.01),
        'f_w_down': nrm(ks[19], (DEPTH, D_FF, D_MODEL), D_FF ** -0.5 * beta),
        'ln2_g': gain(ks[20], (DEPTH, D_MODEL)),
        'ln2_b': nrm(ks[21], (DEPTH, D_MODEL), 0.01),
    }


def reference(x, a_w_in, a_g_q, a_g_kv, a_w_q_lat, a_w_q_idx, a_g_kidx, a_b_kidx, a_w_uv, a_w_out,
              b_w_in, b_lb_logits, b_g_o, b_w_out,
              ln1_g, ln1_b, f_w_up, f_conv_w, f_conv_b, f_w_down, ln2_g, ln2_b):
    c = jnp.cumsum(jax.nn.softmax(b_lb_logits.astype(jnp.float32), axis=0), axis=0)
    lower_bounds = c - c[0:1]
    for layer in range(DEPTH):
        j = layer // N_MIXERS
        if layer % N_MIXERS == 0:
            h = dsa_mixer(x, a_w_in[j], a_g_q[j], a_g_kv[j], a_w_q_lat[j], a_w_q_idx[j],
                          a_g_kidx[j], a_b_kidx[j], a_w_uv[j], a_w_out[j])
        else:
            h = hgrn2_mixer(x, b_w_in[j], lower_bounds[layer], b_g_o[j], b_w_out[j])
        x = layer_norm(DN_ALPHA * x + h, ln1_g[layer], ln1_b[layer])
        h = conv_ffn(x, f_w_up[layer], f_conv_w[layer], f_conv_b[layer], f_w_down[layer])
        x = layer_norm(DN_ALPHA * x + h, ln2_g[layer], ln2_b[layer])
    return x
```

```python
import functools

import jax
import jax.numpy as jnp
from jax import lax
from jax.experimental import pallas as pl
from jax.experimental.pallas import tpu as pltpu

D_MODEL = 1024
DEPTH = 4
N_MIXERS = 2
A_HEADS = 8
A_HEAD_DIM = D_MODEL // A_HEADS
A_Q_RANK = 384
A_KV_RANK = 256
IDX_HEADS = 8
IDX_DIM = 64
TOPK_MAX = 256
B_HEADS = 8
B_DIM = D_MODEL // B_HEADS
D_FF = 2816
DN_ALPHA = (2 * DEPTH) ** 0.25
LN_EPS = 1e-5
RMS_EPS = 1e-6

LANES = 128
SUBLANES = 8
VMEM_LIMIT_BYTES = 56 * 1024 * 1024

MXU_DTYPE = jnp.bfloat16

IDX_PAD = LANES
TQ = 256
HG_T = 256
HG_LEVELS = (8, 16, 32, 64, 128)
FFN_TM = 512
FFN_HALO = SUBLANES
FFN_CF = 1408

SCORE_MASKED = -3.0e38
BIAS_MASKED = -1.0e30


def _dot(a, b):
    return jnp.dot(a, b, preferred_element_type=jnp.float32)


def _dot_nt(a, b):
    return lax.dot_general(a, b, (((1,), (1,)), ((), ())), preferred_element_type=jnp.float32)


def _dot_tn(a, b):
    return lax.dot_general(a, b, (((0,), (0,)), ((), ())), preferred_element_type=jnp.float32)


def _layer_norm(v, g, b):
    mu = jnp.mean(v, axis=-1, keepdims=True)
    c = v - mu
    var = jnp.mean(c * c, axis=-1, keepdims=True)
    return c * lax.rsqrt(var + LN_EPS) * g + b


def _sigmoid(v):
    return 1.0 / (1.0 + jnp.exp(-v))


def _ffn_kernel(xh_ref, x_ref, wa_ref, wu_ref, cwa_ref, cwu_ref, cba_ref, cbu_ref, wd_ref, g_ref, b_ref,
                o_ref, xb_ref, acc_ref, *, tiles_per_seq):
    i = pl.program_id(0)
    j = pl.program_id(1)

    @pl.when(j == 0)
    def _():
        first = (i % tiles_per_seq) == 0
        halo = jnp.where(first, 0.0, xh_ref[...])
        xb_ref[0:FFN_HALO, :] = halo.astype(xb_ref.dtype)
        xb_ref[FFN_HALO:, :] = x_ref[...].astype(xb_ref.dtype)
        acc_ref[...] = jnp.zeros_like(acc_ref)

    xb = xb_ref[...]

    def conv(h, cw_ref, cb_ref):
        h1 = pltpu.roll(h, 1, axis=0)
        h2 = pltpu.roll(h, 2, axis=0)
        y = h * cw_ref[2:3, :] + h1 * cw_ref[1:2, :] + h2 * cw_ref[0:1, :] + cb_ref[...]
        return y[FFN_HALO:, :]

    a = conv(_dot(xb, wa_ref[...]), cwa_ref, cba_ref)
    u = conv(_dot(xb, wu_ref[...]), cwu_ref, cbu_ref)
    gated = (a * _sigmoid(a)) * u
    acc_ref[...] += _dot(gated.astype(wd_ref.dtype), wd_ref[...])

    @pl.when(j == pl.num_programs(1) - 1)
    def _():
        o_ref[...] = _layer_norm(DN_ALPHA * x_ref[...] + acc_ref[...], g_ref[...], b_ref[...])


def _ffn_layer(x2d, seq, w_up, conv_w, conv_b, w_down, ln_g, ln_b):
    n = x2d.shape[0]
    tm = FFN_TM
    n_ff = D_FF // FFN_CF
    wa = w_up[:, :D_FF].astype(MXU_DTYPE)
    wu = w_up[:, D_FF:].astype(MXU_DTYPE)
    cw = conv_w[:, 0, :]
    cwa, cwu = cw[:, :D_FF], cw[:, D_FF:]
    cba, cbu = conv_b[None, :D_FF], conv_b[None, D_FF:]
    wd = w_down.astype(MXU_DTYPE)
    halo_blocks = tm // FFN_HALO
    kern = functools.partial(_ffn_kernel, tiles_per_seq=seq // tm)
    return pl.pallas_call(
        kern,
        grid=(n // tm, n_ff),
        in_specs=[
            pl.BlockSpec((FFN_HALO, D_MODEL), lambda i, j: (jnp.maximum(i * halo_blocks - 1, 0), 0)),
            pl.BlockSpec((tm, D_MODEL), lambda i, j: (i, 0)),
            pl.BlockSpec((D_MODEL, FFN_CF), lambda i, j: (0, j)),
            pl.BlockSpec((D_MODEL, FFN_CF), lambda i, j: (0, j)),
            pl.BlockSpec((3, FFN_CF), lambda i, j: (0, j)),
            pl.BlockSpec((3, FFN_CF), lambda i, j: (0, j)),
            pl.BlockSpec((1, FFN_CF), lambda i, j: (0, j)),
            pl.BlockSpec((1, FFN_CF), lambda i, j: (0, j)),
            pl.BlockSpec((FFN_CF, D_MODEL), lambda i, j: (j, 0)),
            pl.BlockSpec((1, D_MODEL), lambda i, j: (0, 0)),
            pl.BlockSpec((1, D_MODEL), lambda i, j: (0, 0)),
        ],
        out_specs=pl.BlockSpec((tm, D_MODEL), lambda i, j: (i, 0)),
        out_shape=jax.ShapeDtypeStruct((n, D_MODEL), jnp.float32),
        scratch_shapes=[
            pltpu.VMEM((tm + FFN_HALO, D_MODEL), MXU_DTYPE),
            pltpu.VMEM((tm, D_MODEL), jnp.float32),
        ],
        compiler_params=pltpu.CompilerParams(
            dimension_semantics=("arbitrary", "arbitrary"), vmem_limit_bytes=VMEM_LIMIT_BYTES),
        name="ffn",
    )(x2d, x2d, wa, wu, cwa, cwu, cba, cbu, wd, ln_g[None, :], ln_b[None, :])


_PQ0, _PQ1 = 0, A_Q_RANK
_PKV0, _PKV1 = _PQ1, _PQ1 + A_KV_RANK
_PKI0, _PKI1 = _PKV1, _PKV1 + LANES
_PWI0, _PWI1 = _PKI1, _PKI1 + LANES


def _dsa_proj_kernel(x_ref, wcat_ref, gq_ref, gkv_ref, gk_ref, bk_ref, wql_ref, wqi_ref,
                     qlat_ref, qidx_ref, ckv_ref, kidx_ref, widx_ref):
    xb = x_ref[...].astype(wcat_ref.dtype)
    proj = _dot(xb, wcat_ref[...])
    pq = proj[:, _PQ0:_PQ1]
    cq = pq * lax.rsqrt(jnp.mean(pq * pq, axis=-1, keepdims=True) + RMS_EPS) * gq_ref[...]
    pkv = proj[:, _PKV0:_PKV1]
    ckv = pkv * lax.rsqrt(jnp.mean(pkv * pkv, axis=-1, keepdims=True) + RMS_EPS) * gkv_ref[...]
    ckv_ref[...] = ckv.astype(ckv_ref.dtype)
    pk = proj[:, _PKI0:_PKI1]
    real = lax.broadcasted_iota(jnp.int32, pk.shape, 1) < IDX_DIM
    mu = jnp.sum(pk, axis=-1, keepdims=True) * (1.0 / IDX_DIM)
    c = jnp.where(real, pk - mu, 0.0)
    var = jnp.sum(c * c, axis=-1, keepdims=True) * (1.0 / IDX_DIM)
    kidx = c * lax.rsqrt(var + LN_EPS) * gk_ref[...] + bk_ref[...]
    kidx_ref[...] = kidx.astype(kidx_ref.dtype)
    widx_ref[...] = proj[:, _PWI0:_PWI1] * (IDX_HEADS ** -0.5 * IDX_DIM ** -0.5)
    cqb = cq.astype(wql_ref.dtype)
    ql = _dot(cqb, wql_ref[...]) * (A_KV_RANK ** -0.5)
    qi = _dot(cqb, wqi_ref[...])
    for h in range(A_HEADS):
        qlat_ref[0, h] = ql[:, h * A_KV_RANK:(h + 1) * A_KV_RANK].astype(qlat_ref.dtype)
    for h in range(IDX_HEADS):
        qidx_ref[0, h] = qi[:, h * IDX_PAD:(h + 1) * IDX_PAD].astype(qidx_ref.dtype)


def _dsa_proj(x2d, w_in, g_q, g_kv, w_q_lat, w_q_idx, g_kidx, b_kidx):
    n = x2d.shape[0]
    nblk = n // TQ
    o1, o2, o3 = A_Q_RANK, A_Q_RANK + A_KV_RANK, A_Q_RANK + A_KV_RANK + IDX_DIM
    zpad = lambda w, cols: jnp.pad(w, ((0, 0), (0, cols - w.shape[1])))
    wcat = jnp.concatenate(
        [w_in[:, :o2], zpad(w_in[:, o2:o3], LANES), zpad(w_in[:, o3:], LANES)], axis=1).astype(MXU_DTYPE)
    pcols = wcat.shape[1]
    gk = zpad(g_kidx[None, :], LANES)
    bk = zpad(b_kidx[None, :], LANES)
    wqi = jnp.pad(w_q_idx.reshape(A_Q_RANK, IDX_HEADS, IDX_DIM), ((0, 0), (0, 0), (0, IDX_PAD - IDX_DIM)))
    wqi = wqi.reshape(A_Q_RANK, IDX_HEADS * IDX_PAD).astype(MXU_DTYPE)
    wql = w_q_lat.astype(MXU_DTYPE)
    full = lambda shape: pl.BlockSpec(shape, lambda i: (0,) * len(shape))
    return pl.pallas_call(
        _dsa_proj_kernel,
        grid=(nblk,),
        in_specs=[
            pl.BlockSpec((TQ, D_MODEL), lambda i: (i, 0)),
            full((D_MODEL, pcols)), full((1, A_Q_RANK)), full((1, A_KV_RANK)), full((1, LANES)), full((1, LANES)),
            full((A_Q_RANK, A_HEADS * A_KV_RANK)), full((A_Q_RANK, IDX_HEADS * IDX_PAD)),
        ],
        out_specs=[
            pl.BlockSpec((1, A_HEADS, TQ, A_KV_RANK), lambda i: (i, 0, 0, 0)),
            pl.BlockSpec((1, IDX_HEADS, TQ, IDX_PAD), lambda i: (i, 0, 0, 0)),
            pl.BlockSpec((TQ, A_KV_RANK), lambda i: (i, 0)),
            pl.BlockSpec((TQ, IDX_PAD), lambda i: (i, 0)),
            pl.BlockSpec((TQ, LANES), lambda i: (i, 0)),
        ],
        out_shape=[
            jax.ShapeDtypeStruct((nblk, A_HEADS, TQ, A_KV_RANK), MXU_DTYPE),
            jax.ShapeDtypeStruct((nblk, IDX_HEADS, TQ, IDX_PAD), MXU_DTYPE),
            jax.ShapeDtypeStruct((n, A_KV_RANK), MXU_DTYPE),
            jax.ShapeDtypeStruct((n, IDX_PAD), MXU_DTYPE),
            jax.ShapeDtypeStruct((n, LANES), jnp.float32),
        ],
        compiler_params=pltpu.CompilerParams(
            dimension_semantics=("arbitrary",), vmem_limit_bytes=VMEM_LIMIT_BYTES),
        name="dsa_proj",
    )(x2d, wcat, g_q[None, :], g_kv[None, :], gk, bk, wql, wqi)


def _key_to_f32(key):
    bits = key ^ ((key >> 31) & jnp.int32(0x7FFFFFFF))
    return lax.bitcast_convert_type(bits, jnp.float32)


def _dsa_attn_kernel(qlat_ref, qidx_ref, widx_ref, ckv_ref, kidx_ref, x_ref, wuv_ref, wout_ref, g_ref, b_ref,
                     o_ref, sc_ref, m_ref, l_ref, acc_ref, *, top_k):
    i = pl.program_id(1)
    n_tiles = i + 1
    rows = A_HEADS * TQ
    kf = jnp.float32(top_k)
    row_id = lax.broadcasted_iota(jnp.int32, (TQ, TQ), 0)
    col_id = lax.broadcasted_iota(jnp.int32, (TQ, TQ), 1)

    qi = qidx_ref[0].reshape(IDX_HEADS * TQ, IDX_PAD)
    w = widx_ref[...]

    def score_tile(j, carry):
        kt = kidx_ref[0, pl.ds(pl.multiple_of(j * TQ, TQ), TQ), :]
        z = _dot_nt(qi, kt)
        s = jnp.zeros((TQ, TQ), jnp.float32)
        for h in range(IDX_HEADS):
            s = s + w[:, h:h + 1] * jnp.maximum(z[h * TQ:(h + 1) * TQ, :], 0.0)
        valid = (j * TQ + col_id) <= (i * TQ + row_id)
        sc_ref[j] = jnp.where(valid, s, SCORE_MASKED)
        return carry

    lax.fori_loop(0, n_tiles, score_tile, 0)

    def count(pred):
        def body(j, acc):
            hit = jnp.where(pred(sc_ref[j]), 1.0, 0.0)
            return acc + hit[:, :LANES] + hit[:, LANES:]
        acc = lax.fori_loop(0, n_tiles, body, jnp.zeros((TQ, LANES), jnp.float32))
        return jnp.sum(acc, axis=1, keepdims=True)

    n_nonneg = count(lambda s: s >= 0.0)
    key0 = jnp.where(n_nonneg >= kf, jnp.int32(0), jnp.int32(-2 ** 31))

    def bit_step(it, key):
        cand = key | lax.shift_left(jnp.int32(1), 30 - it)
        cand_f = _key_to_f32(cand)
        n_ge = count(lambda s: s >= cand_f)
        return jnp.where(n_ge >= kf, cand, key)

    thr = _key_to_f32(lax.fori_loop(0, 31, bit_step, key0))
    need = jnp.where(thr == SCORE_MASKED, 0.0, kf - count(lambda s: s > thr))

    tri = (row_id <= col_id).astype(wuv_ref.dtype)

    def bias_tile(j, seen):
        s = sc_ref[j]
        eq = s == thr
        prefix = _dot(jnp.where(eq, 1.0, 0.0).astype(tri.dtype), tri) + seen
        take = (s > thr) | (eq & (prefix <= need))
        sc_ref[j] = jnp.where(take, 0.0, BIAS_MASKED)
        return prefix[:, TQ - 1:TQ]

    lax.fori_loop(0, n_tiles, bias_tile, jnp.zeros((TQ, 1), jnp.float32))

    q = qlat_ref[0].reshape(rows, A_KV_RANK)
    m_ref[...] = jnp.full(m_ref.shape, BIAS_MASKED, jnp.float32)
    l_ref[...] = jnp.zeros_like(l_ref)
    acc_ref[...] = jnp.zeros_like(acc_ref)

    def attn_tile(j, carry):
        kv = ckv_ref[0, pl.ds(pl.multiple_of(j * TQ, TQ), TQ), :]
        logits = _dot_nt(q, kv).reshape(A_HEADS, TQ, TQ) + sc_ref[j][None, :, :]
        logits = logits.reshape(rows, TQ)
        m_old = m_ref[...]
        m_new = jnp.maximum(m_old, jnp.max(logits, axis=1, keepdims=True))
        p = jnp.exp(logits - m_new)
        alpha = jnp.exp(m_old - m_new)
        l_ref[...] = alpha * l_ref[...] + jnp.sum(p, axis=1, keepdims=True)
        acc_ref[...] = alpha * acc_ref[...] + _dot(p.astype(kv.dtype), kv)
        m_ref[...] = m_new
        return carry

    lax.fori_loop(0, n_tiles, attn_tile, 0)

    o_lat = acc_ref[...] / l_ref[...]
    heads = []
    for h in range(A_HEADS):
        heads.append(_dot(o_lat[h * TQ:(h + 1) * TQ, :].astype(wuv_ref.dtype), wuv_ref[h]))
    o = jnp.concatenate(heads, axis=1)
    mix = _dot(o.astype(wout_ref.dtype), wout_ref[...])
    o_ref[0] = _layer_norm(DN_ALPHA * x_ref[0] + mix, g_ref[...], b_ref[...])


def _dsa_attn(x, qlat, qidx, widx, ckv, kidx, w_uv, w_out, ln_g, ln_b):
    bsz, seq, _ = x.shape
    nq = seq // TQ
    top_k = min(TOPK_MAX, seq // 4)
    kern = functools.partial(_dsa_attn_kernel, top_k=top_k)
    rows = A_HEADS * TQ
    return pl.pallas_call(
        kern,
        grid=(bsz, nq),
        in_specs=[
            pl.BlockSpec((1, A_HEADS, TQ, A_KV_RANK), lambda b, i: (b * nq + i, 0, 0, 0)),
            pl.BlockSpec((1, IDX_HEADS, TQ, IDX_PAD), lambda b, i: (b * nq + i, 0, 0, 0)),
            pl.BlockSpec((TQ, LANES), lambda b, i: (b * nq + i, 0)),
            pl.BlockSpec((1, seq, A_KV_RANK), lambda b, i: (b, 0, 0)),
            pl.BlockSpec((1, seq, IDX_PAD), lambda b, i: (b, 0, 0)),
            pl.BlockSpec((1, TQ, D_MODEL), lambda b, i: (b, i, 0)),
            pl.BlockSpec((A_HEADS, A_KV_RANK, A_HEAD_DIM), lambda b, i: (0, 0, 0)),
            pl.BlockSpec((D_MODEL, D_MODEL), lambda b, i: (0, 0)),
            pl.BlockSpec((1, D_MODEL), lambda b, i: (0, 0)),
            pl.BlockSpec((1, D_MODEL), lambda b, i: (0, 0)),
        ],
        out_specs=pl.BlockSpec((1, TQ, D_MODEL), lambda b, i: (b, i, 0)),
        out_shape=jax.ShapeDtypeStruct((bsz, seq, D_MODEL), jnp.float32),
        scratch_shapes=[
            pltpu.VMEM((nq, TQ, TQ), jnp.float32),
            pltpu.VMEM((rows, 1), jnp.float32),
            pltpu.VMEM((rows, 1), jnp.float32),
            pltpu.VMEM((rows, A_KV_RANK), jnp.float32),
        ],
        compiler_params=pltpu.CompilerParams(
            dimension_semantics=("arbitrary", "arbitrary"), vmem_limit_bytes=VMEM_LIMIT_BYTES),
        name="dsa_attn",
    )(qlat, qidx, widx, ckv.reshape(bsz, seq, A_KV_RANK), kidx.reshape(bsz, seq, IDX_PAD), x,
      w_uv.astype(MXU_DTYPE), w_out.astype(MXU_DTYPE), ln_g[None, :], ln_b[None, :])


def _dsa_layer(x, w_in, g_q, g_kv, w_q_lat, w_q_idx, g_kidx, b_kidx, w_uv, w_out, ln_g, ln_b):
    bsz, seq, _ = x.shape
    qlat, qidx, ckv, kidx, widx = _dsa_proj(x.reshape(bsz * seq, D_MODEL), w_in, g_q, g_kv, w_q_lat, w_q_idx,
                                            g_kidx, b_kidx)
    return _dsa_attn(x, qlat, qidx, widx, ckv, kidx, w_uv, w_out, ln_g, ln_b)


def _hgrn_kernel(x_ref, win_ref, lb_ref, go_ref, wout_ref, g_ref, b_ref, o_ref,
                 st_ref, qp_ref, kp_ref, on_ref):
    t_len = HG_T

    @pl.when(pl.program_id(1) == 0)
    def _():
        st_ref[...] = jnp.zeros_like(st_ref)

    xb = x_ref[0].astype(win_ref.dtype)
    row = lax.broadcasted_iota(jnp.int32, (t_len, B_DIM), 0)
    in_grp = row % SUBLANES
    rr = lax.broadcasted_iota(jnp.int32, (t_len, t_len), 0)
    cc = lax.broadcasted_iota(jnp.int32, (t_len, t_len), 1)
    rc_xor = rr ^ cc
    ones = jnp.ones((B_DIM, B_DIM), win_ref.dtype)

    def head(h, carry):
        p = _dot(xb, win_ref[h])
        q_raw, f_raw, v, g_raw = (p[:, k * B_DIM:(k + 1) * B_DIM] for k in range(4))
        lb = lb_ref[h]
        forget = lb + (1.0 - lb) * _sigmoid(f_raw)
        log_f = jnp.log(forget)
        k = (1.0 - lb) * _sigmoid(-f_raw)
        q = q_raw * _sigmoid(q_raw)

        b = log_f
        sh = 1
        while sh < t_len:
            b = b + jnp.where(row >= sh, pltpu.roll(b, sh, axis=0), 0.0)
            sh *= 2
        b_last = b[t_len - 1:t_len, :]

        st = st_ref[h]
        o = _dot_nt((q * jnp.exp(b)).astype(xb.dtype), st.astype(xb.dtype))
        k_tail = (k * jnp.exp(b_last - b)).astype(xb.dtype)
        st_ref[h] = st * jnp.exp(b_last) + _dot_tn(v.astype(xb.dtype), k_tail)

        prod = jnp.ones_like(forget)
        for j in range(SUBLANES):
            if j == 0:
                kj, vj = k, v
            else:
                kj, vj = pltpu.roll(k, j, axis=0), pltpu.roll(v, j, axis=0)
                prod = prod * (forget if j == 1 else pltpu.roll(forget, j - 1, axis=0))
            term = jnp.where(in_grp >= j, q * kj * prod, 0.0)
            o = o + _dot(term.astype(xb.dtype), ones) * vj

        for li, c in enumerate(HG_LEVELS):
            for m in range(c, t_len, 2 * c):
                ref_b = b[m - 1:m, :]
                qp_ref[li, m:m + c, :] = q[m:m + c, :] * jnp.exp(b[m:m + c, :] - ref_b)
                qp_ref[li, m - c:m, :] = jnp.zeros((c, B_DIM), jnp.float32)
                kp_ref[li, m - c:m, :] = k[m - c:m, :] * jnp.exp(ref_b - b[m - c:m, :])
                kp_ref[li, m:m + c, :] = jnp.zeros((c, B_DIM), jnp.float32)
        a = jnp.zeros((t_len, t_len), jnp.float32)
        for li, c in enumerate(HG_LEVELS):
            al = _dot_nt(qp_ref[li].astype(xb.dtype), kp_ref[li].astype(xb.dtype))
            a = a + jnp.where(rc_xor < 2 * c, al, 0.0)
        o = o + _dot(a.astype(xb.dtype), v.astype(xb.dtype))

        og = o * _sigmoid(g_raw)
        on = og * lax.rsqrt(jnp.mean(og * og, axis=-1, keepdims=True) + RMS_EPS) * go_ref[h]
        on_ref[h] = on.astype(on_ref.dtype)
        return carry

    lax.fori_loop(0, B_HEADS, head, 0)

    mix = jnp.zeros((t_len, D_MODEL), jnp.float32)
    for h in range(B_HEADS):
        mix = mix + _dot(on_ref[h], wout_ref[h])
    o_ref[0] = _layer_norm(DN_ALPHA * x_ref[0] + mix, g_ref[...], b_ref[...])


def _hgrn_layer(x, w_in, lb, g_o, w_out, ln_g, ln_b):
    bsz, seq, _ = x.shape
    win = w_in.reshape(D_MODEL, 4, B_HEADS, B_DIM).transpose(2, 0, 1, 3).reshape(B_HEADS, D_MODEL, 4 * B_DIM)
    n_lv = len(HG_LEVELS)
    return pl.pallas_call(
        _hgrn_kernel,
        grid=(bsz, seq // HG_T),
        in_specs=[
            pl.BlockSpec((1, HG_T, D_MODEL), lambda b, i: (b, i, 0)),
            pl.BlockSpec((B_HEADS, D_MODEL, 4 * B_DIM), lambda b, i: (0, 0, 0)),
            pl.BlockSpec((B_HEADS, 1, B_DIM), lambda b, i: (0, 0, 0)),
            pl.BlockSpec((B_HEADS, 1, B_DIM), lambda b, i: (0, 0, 0)),
            pl.BlockSpec((B_HEADS, B_DIM, D_MODEL), lambda b, i: (0, 0, 0)),
            pl.BlockSpec((1, D_MODEL), lambda b, i: (0, 0)),
            pl.BlockSpec((1, D_MODEL), lambda b, i: (0, 0)),
        ],
        out_specs=pl.BlockSpec((1, HG_T, D_MODEL), lambda b, i: (b, i, 0)),
        out_shape=jax.ShapeDtypeStruct((bsz, seq, D_MODEL), jnp.float32),
        scratch_shapes=[
            pltpu.VMEM((B_HEADS, B_DIM, B_DIM), jnp.float32),
            pltpu.VMEM((n_lv, HG_T, B_DIM), jnp.float32),
            pltpu.VMEM((n_lv, HG_T, B_DIM), jnp.float32),
            pltpu.VMEM((B_HEADS, HG_T, B_DIM), MXU_DTYPE),
        ],
        compiler_params=pltpu.CompilerParams(
            dimension_semantics=("arbitrary", "arbitrary"), vmem_limit_bytes=VMEM_LIMIT_BYTES),
        name="hgrn",
    )(x, win.astype(MXU_DTYPE), lb.reshape(B_HEADS, 1, B_DIM), g_o.reshape(B_HEADS, 1, B_DIM),
      w_out.reshape(B_HEADS, B_DIM, D_MODEL).astype(MXU_DTYPE), ln_g[None, :], ln_b[None, :])


@jax.jit
def kernel(x, a_w_in, a_g_q, a_g_kv, a_w_q_lat, a_w_q_idx, a_g_kidx, a_b_kidx, a_w_uv, a_w_out, b_w_in, b_lb_logits, b_g_o, b_w_out, ln1_g, ln1_b, f_w_up, f_conv_w, f_conv_b, f_w_down, ln2_g, ln2_b):
    bsz, seq, _ = x.shape
    c = jnp.cumsum(jax.nn.softmax(b_lb_logits.astype(jnp.float32), axis=0), axis=0)
    lower_bounds = c - c[0:1]
    for layer in range(DEPTH):
        j = layer // N_MIXERS
        if layer % N_MIXERS == 0:
            x = _dsa_layer(x, a_w_in[j], a_g_q[j], a_g_kv[j], a_w_q_lat[j], a_w_q_idx[j], a_g_kidx[j],
                           a_b_kidx[j], a_w_uv[j], a_w_out[j], ln1_g[layer], ln1_b[layer])
        else:
            x = _hgrn_layer(x, b_w_in[j], lower_bounds[layer], b_g_o[j], b_w_out[j], ln1_g[layer], ln1_b[layer])
        x = _ffn_layer(x.reshape(bsz * seq, D_MODEL), seq, f_w_up[layer], f_conv_w[layer], f_conv_b[layer],
                       f_w_down[layer], ln2_g[layer], ln2_b[layer]).reshape(bsz, seq, D_MODEL)
    return x
```

```python
import functools

import jax
import jax.numpy as jnp
from jax import lax
from jax.experimental import pallas as pl
from jax.experimental.pallas import tpu as pltpu

D_MODEL = 1024
DEPTH = 4
N_MIXERS = 2
A_HEADS = 8
A_HEAD_DIM = D_MODEL // A_HEADS
A_Q_RANK = 384
A_KV_RANK = 256
IDX_HEADS = 8
IDX_DIM = 64
TOPK_MAX = 256
B_HEADS = 8
B_DIM = D_MODEL // B_HEADS
D_FF = 2816
DN_ALPHA = (2 * DEPTH) ** 0.25
LN_EPS = 1e-5
RMS_EPS = 1e-6

LANES = 128
SUBLANES = 8
VMEM_LIMIT_BYTES = 56 * 1024 * 1024

MXU_DTYPE = jnp.bfloat16

IDX_PAD = LANES
TQ = 256
ATTN_SPAN = 2
HG_T = 256
HG_LEVELS = (8, 16, 32, 64, 128)
FFN_TM = 512
FFN_HALO = SUBLANES
FFN_CF = 1408

SCORE_MASKED = -3.0e38
BIAS_MASKED = -1.0e30


def _dot(a, b):
    return jnp.dot(a, b, preferred_element_type=jnp.float32)


def _dot_nt(a, b):
    return lax.dot_general(a, b, (((1,), (1,)), ((), ())), preferred_element_type=jnp.float32)


def _dot_tn(a, b):
    return lax.dot_general(a, b, (((0,), (0,)), ((), ())), preferred_element_type=jnp.float32)


def _layer_norm(v, g, b):
    mu = jnp.mean(v, axis=-1, keepdims=True)
    c = v - mu
    var = jnp.mean(c * c, axis=-1, keepdims=True)
    return c * lax.rsqrt(var + LN_EPS) * g + b


def _sigmoid(v):
    return 1.0 / (1.0 + jnp.exp(-v))


def _ffn_kernel(xh_ref, x_ref, wa_ref, wu_ref, cwa_ref, cwu_ref, cba_ref, cbu_ref, wd_ref, g_ref, b_ref,
                o_ref, xb_ref, acc_ref, *, tiles_per_seq):
    i = pl.program_id(0)
    j = pl.program_id(1)

    @pl.when(j == 0)
    def _():
        first = (i % tiles_per_seq) == 0
        halo = jnp.where(first, 0.0, xh_ref[...])
        xb_ref[0:FFN_HALO, :] = halo.astype(xb_ref.dtype)
        xb_ref[FFN_HALO:, :] = x_ref[...].astype(xb_ref.dtype)
        acc_ref[...] = jnp.zeros_like(acc_ref)

    xb = xb_ref[...]

    def conv(h, cw_ref, cb_ref):
        h1 = pltpu.roll(h, 1, axis=0)
        h2 = pltpu.roll(h, 2, axis=0)
        y = h * cw_ref[2:3, :] + h1 * cw_ref[1:2, :] + h2 * cw_ref[0:1, :] + cb_ref[...]
        return y[FFN_HALO:, :]

    a = conv(_dot(xb, wa_ref[...]), cwa_ref, cba_ref)
    u = conv(_dot(xb, wu_ref[...]), cwu_ref, cbu_ref)
    gated = (a * _sigmoid(a)) * u
    acc_ref[...] += _dot(gated.astype(wd_ref.dtype), wd_ref[...])

    @pl.when(j == pl.num_programs(1) - 1)
    def _():
        o_ref[...] = _layer_norm(DN_ALPHA * x_ref[...] + acc_ref[...], g_ref[...], b_ref[...])


def _ffn_layer(x2d, seq, w_up, conv_w, conv_b, w_down, ln_g, ln_b):
    n = x2d.shape[0]
    tm = FFN_TM
    n_ff = D_FF // FFN_CF
    wa = w_up[:, :D_FF].astype(MXU_DTYPE)
    wu = w_up[:, D_FF:].astype(MXU_DTYPE)
    cw = conv_w[:, 0, :]
    cwa, cwu = cw[:, :D_FF], cw[:, D_FF:]
    cba, cbu = conv_b[None, :D_FF], conv_b[None, D_FF:]
    wd = w_down.astype(MXU_DTYPE)
    halo_blocks = tm // FFN_HALO
    kern = functools.partial(_ffn_kernel, tiles_per_seq=seq // tm)
    return pl.pallas_call(
        kern,
        grid=(n // tm, n_ff),
        in_specs=[
            pl.BlockSpec((FFN_HALO, D_MODEL), lambda i, j: (jnp.maximum(i * halo_blocks - 1, 0), 0)),
            pl.BlockSpec((tm, D_MODEL), lambda i, j: (i, 0)),
            pl.BlockSpec((D_MODEL, FFN_CF), lambda i, j: (0, j)),
            pl.BlockSpec((D_MODEL, FFN_CF), lambda i, j: (0, j)),
            pl.BlockSpec((3, FFN_CF), lambda i, j: (0, j)),
            pl.BlockSpec((3, FFN_CF), lambda i, j: (0, j)),
            pl.BlockSpec((1, FFN_CF), lambda i, j: (0, j)),
            pl.BlockSpec((1, FFN_CF), lambda i, j: (0, j)),
            pl.BlockSpec((FFN_CF, D_MODEL), lambda i, j: (j, 0)),
            pl.BlockSpec((1, D_MODEL), lambda i, j: (0, 0)),
            pl.BlockSpec((1, D_MODEL), lambda i, j: (0, 0)),
        ],
        out_specs=pl.BlockSpec((tm, D_MODEL), lambda i, j: (i, 0)),
        out_shape=jax.ShapeDtypeStruct((n, D_MODEL), jnp.float32),
        scratch_shapes=[
            pltpu.VMEM((tm + FFN_HALO, D_MODEL), MXU_DTYPE),
            pltpu.VMEM((tm, D_MODEL), jnp.float32),
        ],
        compiler_params=pltpu.CompilerParams(
            dimension_semantics=("arbitrary", "arbitrary"), vmem_limit_bytes=VMEM_LIMIT_BYTES),
        name="ffn",
    )(x2d, x2d, wa, wu, cwa, cwu, cba, cbu, wd, ln_g[None, :], ln_b[None, :])


_PQ0, _PQ1 = 0, A_Q_RANK
_PKV0, _PKV1 = _PQ1, _PQ1 + A_KV_RANK
_PKI0, _PKI1 = _PKV1, _PKV1 + LANES
_PWI0, _PWI1 = _PKI1, _PKI1 + LANES


def _dsa_proj_kernel(x_ref, wcat_ref, gq_ref, gkv_ref, gk_ref, bk_ref, wql_ref, wqi_ref,
                     qlat_ref, qidx_ref, ckv_ref, kidx_ref, widx_ref):
    xb = x_ref[...].astype(wcat_ref.dtype)
    proj = _dot(xb, wcat_ref[...])
    pq = proj[:, _PQ0:_PQ1]
    cq = pq * lax.rsqrt(jnp.mean(pq * pq, axis=-1, keepdims=True) + RMS_EPS) * gq_ref[...]
    pkv = proj[:, _PKV0:_PKV1]
    ckv = pkv * lax.rsqrt(jnp.mean(pkv * pkv, axis=-1, keepdims=True) + RMS_EPS) * gkv_ref[...]
    ckv_ref[...] = ckv.astype(ckv_ref.dtype)
    pk = proj[:, _PKI0:_PKI1]
    real = lax.broadcasted_iota(jnp.int32, pk.shape, 1) < IDX_DIM
    mu = jnp.sum(pk, axis=-1, keepdims=True) * (1.0 / IDX_DIM)
    c = jnp.where(real, pk - mu, 0.0)
    var = jnp.sum(c * c, axis=-1, keepdims=True) * (1.0 / IDX_DIM)
    kidx = c * lax.rsqrt(var + LN_EPS) * gk_ref[...] + bk_ref[...]
    kidx_ref[...] = kidx.astype(kidx_ref.dtype)
    widx_ref[...] = proj[:, _PWI0:_PWI1] * (IDX_HEADS ** -0.5 * IDX_DIM ** -0.5)
    cqb = cq.astype(wql_ref.dtype)
    ql = _dot(cqb, wql_ref[...]) * (A_KV_RANK ** -0.5)
    qi = _dot(cqb, wqi_ref[...])
    for h in range(A_HEADS):
        qlat_ref[0, h] = ql[:, h * A_KV_RANK:(h + 1) * A_KV_RANK].astype(qlat_ref.dtype)
    for h in range(IDX_HEADS):
        qidx_ref[0, h] = qi[:, h * IDX_PAD:(h + 1) * IDX_PAD].astype(qidx_ref.dtype)


def _dsa_proj(x2d, w_in, g_q, g_kv, w_q_lat, w_q_idx, g_kidx, b_kidx):
    n = x2d.shape[0]
    nblk = n // TQ
    o1, o2, o3 = A_Q_RANK, A_Q_RANK + A_KV_RANK, A_Q_RANK + A_KV_RANK + IDX_DIM
    zpad = lambda w, cols: jnp.pad(w, ((0, 0), (0, cols - w.shape[1])))
    wcat = jnp.concatenate(
        [w_in[:, :o2], zpad(w_in[:, o2:o3], LANES), zpad(w_in[:, o3:], LANES)], axis=1).astype(MXU_DTYPE)
    pcols = wcat.shape[1]
    gk = zpad(g_kidx[None, :], LANES)
    bk = zpad(b_kidx[None, :], LANES)
    wqi = jnp.pad(w_q_idx.reshape(A_Q_RANK, IDX_HEADS, IDX_DIM), ((0, 0), (0, 0), (0, IDX_PAD - IDX_DIM)))
    wqi = wqi.reshape(A_Q_RANK, IDX_HEADS * IDX_PAD).astype(MXU_DTYPE)
    wql = w_q_lat.astype(MXU_DTYPE)
    full = lambda shape: pl.BlockSpec(shape, lambda i: (0,) * len(shape))
    return pl.pallas_call(
        _dsa_proj_kernel,
        grid=(nblk,),
        in_specs=[
            pl.BlockSpec((TQ, D_MODEL), lambda i: (i, 0)),
            full((D_MODEL, pcols)), full((1, A_Q_RANK)), full((1, A_KV_RANK)), full((1, LANES)), full((1, LANES)),
            full((A_Q_RANK, A_HEADS * A_KV_RANK)), full((A_Q_RANK, IDX_HEADS * IDX_PAD)),
        ],
        out_specs=[
            pl.BlockSpec((1, A_HEADS, TQ, A_KV_RANK), lambda i: (i, 0, 0, 0)),
            pl.BlockSpec((1, IDX_HEADS, TQ, IDX_PAD), lambda i: (i, 0, 0, 0)),
            pl.BlockSpec((TQ, A_KV_RANK), lambda i: (i, 0)),
            pl.BlockSpec((TQ, IDX_PAD), lambda i: (i, 0)),
            pl.BlockSpec((TQ, LANES), lambda i: (i, 0)),
        ],
        out_shape=[
            jax.ShapeDtypeStruct((nblk, A_HEADS, TQ, A_KV_RANK), MXU_DTYPE),
            jax.ShapeDtypeStruct((nblk, IDX_HEADS, TQ, IDX_PAD), MXU_DTYPE),
            jax.ShapeDtypeStruct((n, A_KV_RANK), MXU_DTYPE),
            jax.ShapeDtypeStruct((n, IDX_PAD), MXU_DTYPE),
            jax.ShapeDtypeStruct((n, LANES), jnp.float32),
        ],
        compiler_params=pltpu.CompilerParams(
            dimension_semantics=("arbitrary",), vmem_limit_bytes=VMEM_LIMIT_BYTES),
        name="dsa_proj",
    )(x2d, wcat, g_q[None, :], g_kv[None, :], gk, bk, wql, wqi)


def _key_to_f32(key):
    bits = key ^ ((key >> 31) & jnp.int32(0x7FFFFFFF))
    return lax.bitcast_convert_type(bits, jnp.float32)


def _dsa_attn_kernel(qlat_ref, qidx_ref, widx_ref, ckv_ref, kidx_ref, x_ref, wuv_ref, wout_ref, g_ref, b_ref,
                     o_ref, sc_ref, m_ref, l_ref, alpha_ref, acc_ref, lg_ref, p_ref, *, top_k):
    i = pl.program_id(1)
    n_tiles = i + 1
    rows = A_HEADS * TQ
    kf = jnp.float32(top_k)
    row_id = lax.broadcasted_iota(jnp.int32, (TQ, TQ), 0)
    col_id = lax.broadcasted_iota(jnp.int32, (TQ, TQ), 1)

    qi = qidx_ref[0].reshape(IDX_HEADS * TQ, IDX_PAD)
    w = widx_ref[...]

    def score_tile(j, carry):
        kt = kidx_ref[0, pl.ds(pl.multiple_of(j * TQ, TQ), TQ), :]
        z = _dot_nt(qi, kt)
        s = jnp.zeros((TQ, TQ), jnp.float32)
        for h in range(IDX_HEADS):
            s = s + w[:, h:h + 1] * jnp.maximum(z[h * TQ:(h + 1) * TQ, :], 0.0)
        valid = (j * TQ + col_id) <= (i * TQ + row_id)
        sc_ref[j] = jnp.where(valid, s, SCORE_MASKED)
        return carry

    lax.fori_loop(0, n_tiles, score_tile, 0)

    def count(pred):
        def body(j, acc):
            hit = jnp.where(pred(sc_ref[j]), 1.0, 0.0)
            return acc + hit[:, :LANES] + hit[:, LANES:]
        acc = lax.fori_loop(0, n_tiles, body, jnp.zeros((TQ, LANES), jnp.float32))
        return jnp.sum(acc, axis=1, keepdims=True)

    n_nonneg = count(lambda s: s >= 0.0)
    key0 = jnp.where(n_nonneg >= kf, jnp.int32(0), jnp.int32(-2 ** 31))

    def bit_step(it, key):
        cand = key | lax.shift_left(jnp.int32(1), 30 - it)
        cand_f = _key_to_f32(cand)
        n_ge = count(lambda s: s >= cand_f)
        return jnp.where(n_ge >= kf, cand, key)

    thr = _key_to_f32(lax.fori_loop(0, 31, bit_step, key0))
    need = jnp.where(thr == SCORE_MASKED, 0.0, kf - count(lambda s: s > thr))

    tri = (row_id <= col_id).astype(wuv_ref.dtype)

    def bias_tile(j, seen):
        s = sc_ref[j]
        eq = s == thr
        prefix = _dot(jnp.where(eq, 1.0, 0.0).astype(tri.dtype), tri) + seen
        take = (s > thr) | (eq & (prefix <= need))
        sc_ref[j] = jnp.where(take, 0.0, BIAS_MASKED)
        return prefix[:, TQ - 1:TQ]

    lax.fori_loop(0, n_tiles, bias_tile, jnp.zeros((TQ, 1), jnp.float32))

    @pl.when(i % ATTN_SPAN == 0)
    def _():
        sc_ref[i + 1] = jnp.full((TQ, TQ), BIAS_MASKED, jnp.float32)

    q = qlat_ref[0].reshape(rows, A_KV_RANK)
    m_ref[...] = jnp.full(m_ref.shape, BIAS_MASKED, jnp.float32)
    l_ref[...] = jnp.zeros_like(l_ref)
    acc_ref[...] = jnp.zeros_like(acc_ref)
    tk = ATTN_SPAN * TQ

    def lanes(v, width):
        return jnp.concatenate([v] * (width // LANES), axis=1)

    def attn_tile(jj, carry):
        kv = ckv_ref[0, pl.ds(pl.multiple_of(jj * tk, tk), tk), :]
        lg_ref[...] = _dot_nt(q, kv)
        bias = jnp.concatenate([sc_ref[ATTN_SPAN * jj + u] for u in range(ATTN_SPAN)], axis=1)
        for h in range(A_HEADS):
            rs = slice(h * TQ, (h + 1) * TQ)
            logits = lg_ref[rs, :] + bias
            m_old = m_ref[rs, :]
            m_new = jnp.maximum(m_old, jnp.max(logits, axis=1, keepdims=True))
            p = jnp.exp(logits - lanes(m_new, tk))
            alpha = jnp.exp(m_old - m_new)
            l_ref[rs, :] = alpha * l_ref[rs, :] + jnp.sum(p, axis=1, keepdims=True)
            m_ref[rs, :] = m_new
            alpha_ref[rs, :] = alpha
            p_ref[rs, :] = p.astype(p_ref.dtype)
        acc_ref[...] = lanes(alpha_ref[...], A_KV_RANK) * acc_ref[...] + _dot(p_ref[...], kv)
        return carry

    lax.fori_loop(0, (i + ATTN_SPAN) // ATTN_SPAN, attn_tile, 0)

    o_lat = acc_ref[...] / lanes(l_ref[...], A_KV_RANK)
    heads = []
    for h in range(A_HEADS):
        heads.append(_dot(o_lat[h * TQ:(h + 1) * TQ, :].astype(wuv_ref.dtype), wuv_ref[h]))
    o = jnp.concatenate(heads, axis=1)
    mix = _dot(o.astype(wout_ref.dtype), wout_ref[...])
    o_ref[0] = _layer_norm(DN_ALPHA * x_ref[0] + mix, g_ref[...], b_ref[...])


def _dsa_attn(x, qlat, qidx, widx, ckv, kidx, w_uv, w_out, ln_g, ln_b):
    bsz, seq, _ = x.shape
    nq = seq // TQ
    top_k = min(TOPK_MAX, seq // 4)
    kern = functools.partial(_dsa_attn_kernel, top_k=top_k)
    rows = A_HEADS * TQ
    return pl.pallas_call(
        kern,
        grid=(bsz, nq),
        in_specs=[
            pl.BlockSpec((1, A_HEADS, TQ, A_KV_RANK), lambda b, i: (b * nq + i, 0, 0, 0)),
            pl.BlockSpec((1, IDX_HEADS, TQ, IDX_PAD), lambda b, i: (b * nq + i, 0, 0, 0)),
            pl.BlockSpec((TQ, LANES), lambda b, i: (b * nq + i, 0)),
            pl.BlockSpec((1, seq, A_KV_RANK), lambda b, i: (b, 0, 0)),
            pl.BlockSpec((1, seq, IDX_PAD), lambda b, i: (b, 0, 0)),
            pl.BlockSpec((1, TQ, D_MODEL), lambda b, i: (b, i, 0)),
            pl.BlockSpec((A_HEADS, A_KV_RANK, A_HEAD_DIM), lambda b, i: (0, 0, 0)),
            pl.BlockSpec((D_MODEL, D_MODEL), lambda b, i: (0, 0)),
            pl.BlockSpec((1, D_MODEL), lambda b, i: (0, 0)),
            pl.BlockSpec((1, D_MODEL), lambda b, i: (0, 0)),
        ],
        out_specs=pl.BlockSpec((1, TQ, D_MODEL), lambda b, i: (b, i, 0)),
        out_shape=jax.ShapeDtypeStruct((bsz, seq, D_MODEL), jnp.float32),
        scratch_shapes=[
            pltpu.VMEM((nq, TQ, TQ), jnp.float32),
            pltpu.VMEM((rows, LANES), jnp.float32),
            pltpu.VMEM((rows, LANES), jnp.float32),
            pltpu.VMEM((rows, LANES), jnp.float32),
            pltpu.VMEM((rows, A_KV_RANK), jnp.float32),
            pltpu.VMEM((rows, ATTN_SPAN * TQ), jnp.float32),
            pltpu.VMEM((rows, ATTN_SPAN * TQ), MXU_DTYPE),
        ],
        compiler_params=pltpu.CompilerParams(
            dimension_semantics=("arbitrary", "arbitrary"), vmem_limit_bytes=VMEM_LIMIT_BYTES),
        name="dsa_attn",
    )(qlat, qidx, widx, ckv.reshape(bsz, seq, A_KV_RANK), kidx.reshape(bsz, seq, IDX_PAD), x,
      w_uv.astype(MXU_DTYPE), w_out.astype(MXU_DTYPE), ln_g[None, :], ln_b[None, :])


def _dsa_layer(x, w_in, g_q, g_kv, w_q_lat, w_q_idx, g_kidx, b_kidx, w_uv, w_out, ln_g, ln_b):
    bsz, seq, _ = x.shape
    qlat, qidx, ckv, kidx, widx = _dsa_proj(x.reshape(bsz * seq, D_MODEL), w_in, g_q, g_kv, w_q_lat, w_q_idx,
                                            g_kidx, b_kidx)
    return _dsa_attn(x, qlat, qidx, widx, ckv, kidx, w_uv, w_out, ln_g, ln_b)


def _hgrn_kernel(x_ref, win_ref, lb_ref, go_ref, wout_ref, g_ref, b_ref, o_ref,
                 st_ref, qp_ref, kp_ref, on_ref):
    t_len = HG_T

    @pl.when(pl.program_id(1) == 0)
    def _():
        st_ref[...] = jnp.zeros_like(st_ref)

    xb = x_ref[0].astype(win_ref.dtype)
    row = lax.broadcasted_iota(jnp.int32, (t_len, B_DIM), 0)
    in_grp = row % SUBLANES
    rr = lax.broadcasted_iota(jnp.int32, (t_len, t_len), 0)
    cc = lax.broadcasted_iota(jnp.int32, (t_len, t_len), 1)
    rc_xor = rr ^ cc
    ones = jnp.ones((B_DIM, B_DIM), win_ref.dtype)

    def head(h, carry):
        p = _dot(xb, win_ref[h])
        q_raw, f_raw, v, g_raw = (p[:, k * B_DIM:(k + 1) * B_DIM] for k in range(4))
        lb = lb_ref[h]
        forget = lb + (1.0 - lb) * _sigmoid(f_raw)
        log_f = jnp.log(forget)
        k = (1.0 - lb) * _sigmoid(-f_raw)
        q = q_raw * _sigmoid(q_raw)

        b = log_f
        sh = 1
        while sh < t_len:
            b = b + jnp.where(row >= sh, pltpu.roll(b, sh, axis=0), 0.0)
            sh *= 2
        b_last = b[t_len - 1:t_len, :]

        st = st_ref[h]
        o = _dot_nt((q * jnp.exp(b)).astype(xb.dtype), st.astype(xb.dtype))
        k_tail = (k * jnp.exp(b_last - b)).astype(xb.dtype)
        st_ref[h] = st * jnp.exp(b_last) + _dot_tn(v.astype(xb.dtype), k_tail)

        prod = jnp.ones_like(forget)
        for j in range(SUBLANES):
            if j == 0:
                kj, vj = k, v
            else:
                kj, vj = pltpu.roll(k, j, axis=0), pltpu.roll(v, j, axis=0)
                prod = prod * (forget if j == 1 else pltpu.roll(forget, j - 1, axis=0))
            term = jnp.where(in_grp >= j, q * kj * prod, 0.0)
            o = o + _dot(term.astype(xb.dtype), ones) * vj

        for li, c in enumerate(HG_LEVELS):
            for m in range(c, t_len, 2 * c):
                ref_b = b[m - 1:m, :]
                qp_ref[li, m:m + c, :] = q[m:m + c, :] * jnp.exp(b[m:m + c, :] - ref_b)
                qp_ref[li, m - c:m, :] = jnp.zeros((c, B_DIM), jnp.float32)
                kp_ref[li, m - c:m, :] = k[m - c:m, :] * jnp.exp(ref_b - b[m - c:m, :])
                kp_ref[li, m:m + c, :] = jnp.zeros((c, B_DIM), jnp.float32)
        a = jnp.zeros((t_len, t_len), jnp.float32)
        for li, c in enumerate(HG_LEVELS):
            al = _dot_nt(qp_ref[li].astype(xb.dtype), kp_ref[li].astype(xb.dtype))
            a = a + jnp.where(rc_xor < 2 * c, al, 0.0)
        o = o + _dot(a.astype(xb.dtype), v.astype(xb.dtype))

        og = o * _sigmoid(g_raw)
        on = og * lax.rsqrt(jnp.mean(og * og, axis=-1, keepdims=True) + RMS_EPS) * go_ref[h]
        on_ref[h] = on.astype(on_ref.dtype)
        return carry

    lax.fori_loop(0, B_HEADS, head, 0)

    mix = jnp.zeros((t_len, D_MODEL), jnp.float32)
    for h in range(B_HEADS):
        mix = mix + _dot(on_ref[h], wout_ref[h])
    o_ref[0] = _layer_norm(DN_ALPHA * x_ref[0] + mix, g_ref[...], b_ref[...])


def _hgrn_layer(x, w_in, lb, g_o, w_out, ln_g, ln_b):
    bsz, seq, _ = x.shape
    win = w_in.reshape(D_MODEL, 4, B_HEADS, B_DIM).transpose(2, 0, 1, 3).reshape(B_HEADS, D_MODEL, 4 * B_DIM)
    n_lv = len(HG_LEVELS)
    return pl.pallas_call(
        _hgrn_kernel,
        grid=(bsz, seq // HG_T),
        in_specs=[
            pl.BlockSpec((1, HG_T, D_MODEL), lambda b, i: (b, i, 0)),
            pl.BlockSpec((B_HEADS, D_MODEL, 4 * B_DIM), lambda b, i: (0, 0, 0)),
            pl.BlockSpec((B_HEADS, 1, B_DIM), lambda b, i: (0, 0, 0)),
            pl.BlockSpec((B_HEADS, 1, B_DIM), lambda b, i: (0, 0, 0)),
            pl.BlockSpec((B_HEADS, B_DIM, D_MODEL), lambda b, i: (0, 0, 0)),
            pl.BlockSpec((1, D_MODEL), lambda b, i: (0, 0)),
            pl.BlockSpec((1, D_MODEL), lambda b, i: (0, 0)),
        ],
        out_specs=pl.BlockSpec((1, HG_T, D_MODEL), lambda b, i: (b, i, 0)),
        out_shape=jax.ShapeDtypeStruct((bsz, seq, D_MODEL), jnp.float32),
        scratch_shapes=[
            pltpu.VMEM((B_HEADS, B_DIM, B_DIM), jnp.float32),
            pltpu.VMEM((n_lv, HG_T, B_DIM), jnp.float32),
            pltpu.VMEM((n_lv, HG_T, B_DIM), jnp.float32),
            pltpu.VMEM((B_HEADS, HG_T, B_DIM), MXU_DTYPE),
        ],
        compiler_params=pltpu.CompilerParams(
            dimension_semantics=("arbitrary", "arbitrary"), vmem_limit_bytes=VMEM_LIMIT_BYTES),
        name="hgrn",
    )(x, win.astype(MXU_DTYPE), lb.reshape(B_HEADS, 1, B_DIM), g_o.reshape(B_HEADS, 1, B_DIM),
      w_out.reshape(B_HEADS, B_DIM, D_MODEL).astype(MXU_DTYPE), ln_g[None, :], ln_b[None, :])


@jax.jit
def kernel(x, a_w_in, a_g_q, a_g_kv, a_w_q_lat, a_w_q_idx, a_g_kidx, a_b_kidx, a_w_uv, a_w_out, b_w_in, b_lb_logits, b_g_o, b_w_out, ln1_g, ln1_b, f_w_up, f_conv_w, f_conv_b, f_w_down, ln2_g, ln2_b):
    bsz, seq, _ = x.shape
    c = jnp.cumsum(jax.nn.softmax(b_lb_logits.astype(jnp.float32), axis=0), axis=0)
    lower_bounds = c - c[0:1]
    for layer in range(DEPTH):
        j = layer // N_MIXERS
        if layer % N_MIXERS == 0:
            x = _dsa_layer(x, a_w_in[j], a_g_q[j], a_g_kv[j], a_w_q_lat[j], a_w_q_idx[j], a_g_kidx[j],
                           a_b_kidx[j], a_w_uv[j], a_w_out[j], ln1_g[layer], ln1_b[layer])
        else:
            x = _hgrn_layer(x, b_w_in[j], lower_bounds[layer], b_g_o[j], b_w_out[j], ln1_g[layer], ln1_b[layer])
        x = _ffn_layer(x.reshape(bsz * seq, D_MODEL), seq, f_w_up[layer], f_conv_w[layer], f_conv_b[layer],
                       f_w_down[layer], ln2_g[layer], ln2_b[layer]).reshape(bsz, seq, D_MODEL)
    return x
```

```python
import functools

import jax
import jax.numpy as jnp
from jax import lax
from jax.experimental import pallas as pl
from jax.experimental.pallas import tpu as pltpu

D_MODEL = 1024
DEPTH = 4
N_MIXERS = 2
A_HEADS = 8
A_HEAD_DIM = D_MODEL // A_HEADS
A_Q_RANK = 384
A_KV_RANK = 256
IDX_HEADS = 8
IDX_DIM = 64
TOPK_MAX = 256
B_HEADS = 8
B_DIM = D_MODEL // B_HEADS
D_FF = 2816
DN_ALPHA = (2 * DEPTH) ** 0.25
LN_EPS = 1e-5
RMS_EPS = 1e-6
LOG2_E = 1.4426950408889634

LANES = 128
SUBLANES = 8
VMEM_LIMIT_BYTES = 56 * 1024 * 1024

MXU_DTYPE = jnp.bfloat16

IDX_PAD = LANES
TQ = 256
ATTN_SPAN = 2
ATTN_CHUNK = 64
ATTN_UNROLL = 8
HG_T = 256
HG_LEVELS = (8, 16, 32, 64, 128)
FFN_TM = 512
FFN_HALO = SUBLANES
FFN_CF = 1408

COUNT_ROWS = 4 * SUBLANES
SCORE_MASKED = -3.0e38
BIAS_MASKED = -1.0e30


def _dot(a, b):
    return jnp.dot(a, b, preferred_element_type=jnp.float32)


def _dot_nt(a, b):
    return lax.dot_general(a, b, (((1,), (1,)), ((), ())), preferred_element_type=jnp.float32)


def _dot_tn(a, b):
    return lax.dot_general(a, b, (((0,), (0,)), ((), ())), preferred_element_type=jnp.float32)


def _layer_norm(v, g, b):
    mu = jnp.mean(v, axis=-1, keepdims=True)
    c = v - mu
    var = jnp.mean(c * c, axis=-1, keepdims=True)
    return c * lax.rsqrt(var + LN_EPS) * g + b


def _sigmoid(v):
    return 1.0 / (1.0 + jnp.exp(-v))


def _ffn_kernel(xh_ref, x_ref, wa_ref, wu_ref, cwa_ref, cwu_ref, cba_ref, cbu_ref, wd_ref, g_ref, b_ref,
                o_ref, xb_ref, acc_ref, *, tiles_per_seq):
    i = pl.program_id(0)
    j = pl.program_id(1)

    @pl.when(j == 0)
    def _():
        first = (i % tiles_per_seq) == 0
        halo = jnp.where(first, 0.0, xh_ref[...])
        xb_ref[0:FFN_HALO, :] = halo.astype(xb_ref.dtype)
        xb_ref[FFN_HALO:, :] = x_ref[...].astype(xb_ref.dtype)
        acc_ref[...] = jnp.zeros_like(acc_ref)

    xb = xb_ref[...]

    def conv(h, cw_ref, cb_ref):
        h1 = pltpu.roll(h, 1, axis=0)
        h2 = pltpu.roll(h, 2, axis=0)
        y = h * cw_ref[2:3, :] + h1 * cw_ref[1:2, :] + h2 * cw_ref[0:1, :] + cb_ref[...]
        return y[FFN_HALO:, :]

    a = conv(_dot(xb, wa_ref[...]), cwa_ref, cba_ref)
    u = conv(_dot(xb, wu_ref[...]), cwu_ref, cbu_ref)
    gated = (a * _sigmoid(a)) * u
    acc_ref[...] += _dot(gated.astype(wd_ref.dtype), wd_ref[...])

    @pl.when(j == pl.num_programs(1) - 1)
    def _():
        o_ref[...] = _layer_norm(DN_ALPHA * x_ref[...] + acc_ref[...], g_ref[...], b_ref[...])


def _ffn_layer(x2d, seq, w_up, conv_w, conv_b, w_down, ln_g, ln_b):
    n = x2d.shape[0]
    tm = FFN_TM
    n_ff = D_FF // FFN_CF
    wa = w_up[:, :D_FF].astype(MXU_DTYPE)
    wu = w_up[:, D_FF:].astype(MXU_DTYPE)
    cw = conv_w[:, 0, :]
    cwa, cwu = cw[:, :D_FF], cw[:, D_FF:]
    cba, cbu = conv_b[None, :D_FF], conv_b[None, D_FF:]
    wd = w_down.astype(MXU_DTYPE)
    halo_blocks = tm // FFN_HALO
    kern = functools.partial(_ffn_kernel, tiles_per_seq=seq // tm)
    return pl.pallas_call(
        kern,
        grid=(n // tm, n_ff),
        in_specs=[
            pl.BlockSpec((FFN_HALO, D_MODEL), lambda i, j: (jnp.maximum(i * halo_blocks - 1, 0), 0)),
            pl.BlockSpec((tm, D_MODEL), lambda i, j: (i, 0)),
            pl.BlockSpec((D_MODEL, FFN_CF), lambda i, j: (0, j)),
            pl.BlockSpec((D_MODEL, FFN_CF), lambda i, j: (0, j)),
            pl.BlockSpec((3, FFN_CF), lambda i, j: (0, j)),
            pl.BlockSpec((3, FFN_CF), lambda i, j: (0, j)),
            pl.BlockSpec((1, FFN_CF), lambda i, j: (0, j)),
            pl.BlockSpec((1, FFN_CF), lambda i, j: (0, j)),
            pl.BlockSpec((FFN_CF, D_MODEL), lambda i, j: (j, 0)),
            pl.BlockSpec((1, D_MODEL), lambda i, j: (0, 0)),
            pl.BlockSpec((1, D_MODEL), lambda i, j: (0, 0)),
        ],
        out_specs=pl.BlockSpec((tm, D_MODEL), lambda i, j: (i, 0)),
        out_shape=jax.ShapeDtypeStruct((n, D_MODEL), jnp.float32),
        scratch_shapes=[
            pltpu.VMEM((tm + FFN_HALO, D_MODEL), MXU_DTYPE),
            pltpu.VMEM((tm, D_MODEL), jnp.float32),
        ],
        compiler_params=pltpu.CompilerParams(
            dimension_semantics=("arbitrary", "arbitrary"), vmem_limit_bytes=VMEM_LIMIT_BYTES),
        name="ffn",
    )(x2d, x2d, wa, wu, cwa, cwu, cba, cbu, wd, ln_g[None, :], ln_b[None, :])


_PQ0, _PQ1 = 0, A_Q_RANK
_PKV0, _PKV1 = _PQ1, _PQ1 + A_KV_RANK
_PKI0, _PKI1 = _PKV1, _PKV1 + LANES
_PWI0, _PWI1 = _PKI1, _PKI1 + LANES


def _dsa_proj_kernel(x_ref, wcat_ref, gq_ref, gkv_ref, gk_ref, bk_ref, wql_ref, wqi_ref,
                     qlat_ref, qidx_ref, ckv_ref, kidx_ref, widx_ref):
    xb = x_ref[...].astype(wcat_ref.dtype)
    proj = _dot(xb, wcat_ref[...])
    pq = proj[:, _PQ0:_PQ1]
    cq = pq * lax.rsqrt(jnp.mean(pq * pq, axis=-1, keepdims=True) + RMS_EPS) * gq_ref[...]
    pkv = proj[:, _PKV0:_PKV1]
    ckv = pkv * lax.rsqrt(jnp.mean(pkv * pkv, axis=-1, keepdims=True) + RMS_EPS) * gkv_ref[...]
    ckv_ref[...] = ckv.astype(ckv_ref.dtype)
    pk = proj[:, _PKI0:_PKI1]
    real = lax.broadcasted_iota(jnp.int32, pk.shape, 1) < IDX_DIM
    mu = jnp.sum(pk, axis=-1, keepdims=True) * (1.0 / IDX_DIM)
    c = jnp.where(real, pk - mu, 0.0)
    var = jnp.sum(c * c, axis=-1, keepdims=True) * (1.0 / IDX_DIM)
    kidx = c * lax.rsqrt(var + LN_EPS) * gk_ref[...] + bk_ref[...]
    kidx_ref[...] = kidx.astype(kidx_ref.dtype)
    widx_ref[...] = proj[:, _PWI0:_PWI1] * (IDX_HEADS ** -0.5 * IDX_DIM ** -0.5)
    cqb = cq.astype(wql_ref.dtype)
    ql = _dot(cqb, wql_ref[...]) * (A_KV_RANK ** -0.5 * LOG2_E)
    qi = _dot(cqb, wqi_ref[...])
    for h in range(A_HEADS):
        qlat_ref[0, h] = ql[:, h * A_KV_RANK:(h + 1) * A_KV_RANK].astype(qlat_ref.dtype)
    for h in range(IDX_HEADS):
        qidx_ref[0, h] = qi[:, h * IDX_PAD:(h + 1) * IDX_PAD].astype(qidx_ref.dtype)


def _dsa_proj(x2d, w_in, g_q, g_kv, w_q_lat, w_q_idx, g_kidx, b_kidx):
    n = x2d.shape[0]
    nblk = n // TQ
    o1, o2, o3 = A_Q_RANK, A_Q_RANK + A_KV_RANK, A_Q_RANK + A_KV_RANK + IDX_DIM
    zpad = lambda w, cols: jnp.pad(w, ((0, 0), (0, cols - w.shape[1])))
    wcat = jnp.concatenate(
        [w_in[:, :o2], zpad(w_in[:, o2:o3], LANES), zpad(w_in[:, o3:], LANES)], axis=1).astype(MXU_DTYPE)
    pcols = wcat.shape[1]
    gk = zpad(g_kidx[None, :], LANES)
    bk = zpad(b_kidx[None, :], LANES)
    wqi = jnp.pad(w_q_idx.reshape(A_Q_RANK, IDX_HEADS, IDX_DIM), ((0, 0), (0, 0), (0, IDX_PAD - IDX_DIM)))
    wqi = wqi.reshape(A_Q_RANK, IDX_HEADS * IDX_PAD).astype(MXU_DTYPE)
    wql = w_q_lat.astype(MXU_DTYPE)
    full = lambda shape: pl.BlockSpec(shape, lambda i: (0,) * len(shape))
    return pl.pallas_call(
        _dsa_proj_kernel,
        grid=(nblk,),
        in_specs=[
            pl.BlockSpec((TQ, D_MODEL), lambda i: (i, 0)),
            full((D_MODEL, pcols)), full((1, A_Q_RANK)), full((1, A_KV_RANK)), full((1, LANES)), full((1, LANES)),
            full((A_Q_RANK, A_HEADS * A_KV_RANK)), full((A_Q_RANK, IDX_HEADS * IDX_PAD)),
        ],
        out_specs=[
            pl.BlockSpec((1, A_HEADS, TQ, A_KV_RANK), lambda i: (i, 0, 0, 0)),
            pl.BlockSpec((1, IDX_HEADS, TQ, IDX_PAD), lambda i: (i, 0, 0, 0)),
            pl.BlockSpec((TQ, A_KV_RANK), lambda i: (i, 0)),
            pl.BlockSpec((TQ, IDX_PAD), lambda i: (i, 0)),
            pl.BlockSpec((TQ, LANES), lambda i: (i, 0)),
        ],
        out_shape=[
            jax.ShapeDtypeStruct((nblk, A_HEADS, TQ, A_KV_RANK), MXU_DTYPE),
            jax.ShapeDtypeStruct((nblk, IDX_HEADS, TQ, IDX_PAD), MXU_DTYPE),
            jax.ShapeDtypeStruct((n, A_KV_RANK), MXU_DTYPE),
            jax.ShapeDtypeStruct((n, IDX_PAD), MXU_DTYPE),
            jax.ShapeDtypeStruct((n, LANES), jnp.float32),
        ],
        compiler_params=pltpu.CompilerParams(
            dimension_semantics=("arbitrary",), vmem_limit_bytes=VMEM_LIMIT_BYTES),
        name="dsa_proj",
    )(x2d, wcat, g_q[None, :], g_kv[None, :], gk, bk, wql, wqi)


def _key_to_f32(key):
    bits = key ^ ((key >> 31) & jnp.int32(0x7FFFFFFF))
    return lax.bitcast_convert_type(bits, jnp.float32)


def _dsa_attn_kernel(qlat_ref, qidx_ref, widx_ref, ckv_ref, kidx_ref, x_ref, wuv_ref, wout_ref, g_ref, b_ref,
                     o_ref, sc_ref, sct_ref, m_ref, l_ref, alpha_ref, acc_ref, lg_ref, p_ref, *, top_k):
    i = pl.program_id(1)
    n_tiles = i + 1
    rows = A_HEADS * TQ
    kf = jnp.float32(top_k)
    row_id = lax.broadcasted_iota(jnp.int32, (TQ, TQ), 0)
    col_id = lax.broadcasted_iota(jnp.int32, (TQ, TQ), 1)

    qi = qidx_ref[0].reshape(IDX_HEADS * TQ, IDX_PAD)
    w = widx_ref[...]

    def score_tile(j, carry):
        kt = kidx_ref[0, pl.ds(pl.multiple_of(j * TQ, TQ), TQ), :]
        z = _dot_nt(qi, kt)
        s = jnp.zeros((TQ, TQ), jnp.float32)
        for h in range(IDX_HEADS):
            s = s + w[:, h:h + 1] * jnp.maximum(z[h * TQ:(h + 1) * TQ, :], 0.0)
        valid = (j * TQ + col_id) <= (i * TQ + row_id)
        s = jnp.where(valid, s, SCORE_MASKED)
        sc_ref[j] = s
        sct_ref[j] = s.T
        return carry

    lax.fori_loop(0, n_tiles, score_tile, 0)

    def count(pred):
        def body(j, acc):
            hit = jnp.where(pred(sct_ref[j]), 1.0, 0.0)
            return acc + jnp.sum(hit.reshape(TQ // COUNT_ROWS, COUNT_ROWS, TQ), axis=0)
        acc = lax.fori_loop(0, n_tiles, body, jnp.zeros((COUNT_ROWS, TQ), jnp.float32))
        return jnp.sum(acc, axis=0, keepdims=True)

    n_nonneg = count(lambda s: s >= 0.0)
    key0 = jnp.where(n_nonneg >= kf, jnp.int32(0), jnp.int32(-2 ** 31))

    def bit_step(it, key):
        cand = key | lax.shift_left(jnp.int32(1), 30 - it)
        cand_f = _key_to_f32(cand)
        n_ge = count(lambda s: s >= cand_f)
        return jnp.where(n_ge >= kf, cand, key)

    thr = _key_to_f32(lax.fori_loop(0, 31, bit_step, key0))
    need = jnp.where(thr == SCORE_MASKED, 0.0, kf - count(lambda s: s > thr))

    def to_cols(v):
        return jnp.broadcast_to(v, (LANES, TQ)).T

    thr_c = to_cols(thr)
    need_c = to_cols(need)

    tri = (row_id <= col_id).astype(wuv_ref.dtype)
    thr2 = jnp.concatenate([thr_c, thr_c], axis=1)
    need2 = jnp.concatenate([need_c, need_c], axis=1)

    def bias_tile(j, seen):
        s = sc_ref[j]
        eq = s == thr2
        prefix = _dot(jnp.where(eq, 1.0, 0.0).astype(tri.dtype), tri) + seen
        take = (s > thr2) | (eq & (prefix <= need2))
        sc_ref[j] = jnp.where(take, 0.0, BIAS_MASKED)
        return prefix[:, TQ - 1:TQ]

    lax.fori_loop(0, n_tiles, bias_tile, jnp.zeros((TQ, 1), jnp.float32))

    @pl.when(i % ATTN_SPAN == 0)
    def _():
        sc_ref[i + 1] = jnp.full((TQ, TQ), BIAS_MASKED, jnp.float32)

    q = qlat_ref[0].reshape(rows, A_KV_RANK)
    m_ref[...] = jnp.full(m_ref.shape, BIAS_MASKED, jnp.float32)
    l_ref[...] = jnp.zeros_like(l_ref)
    acc_ref[...] = jnp.zeros_like(acc_ref)
    tk = ATTN_SPAN * TQ

    def lanes(v, width):
        return jnp.concatenate([v] * (width // LANES), axis=1)

    def attn_tile(jj, carry):
        kv = ckv_ref[0, pl.ds(pl.multiple_of(jj * tk, tk), tk), :]
        for h in range(A_HEADS):
            hs = slice(h * TQ, (h + 1) * TQ)
            lg_ref[hs, :] = _dot_nt(q[hs], kv)
            for c in range(TQ // ATTN_CHUNK):
                bs = slice(c * ATTN_CHUNK, (c + 1) * ATTN_CHUNK)
                rs = slice(h * TQ + c * ATTN_CHUNK, h * TQ + (c + 1) * ATTN_CHUNK)
                bias = jnp.concatenate([sc_ref[ATTN_SPAN * jj + u, bs, :] for u in range(ATTN_SPAN)], axis=1)
                logits = lg_ref[rs, :] + bias
                m_old = m_ref[rs, :]
                m_new = jnp.maximum(m_old, jnp.max(logits, axis=1, keepdims=True))
                p = jnp.exp2(logits - lanes(m_new, tk))
                alpha = jnp.exp2(m_old - m_new)
                p_lanes = sum(p[:, u * LANES:(u + 1) * LANES] for u in range(tk // LANES))
                l_ref[rs, :] = alpha * l_ref[rs, :] + p_lanes
                m_ref[rs, :] = m_new
                alpha_ref[rs, :] = alpha
                p_ref[rs, :] = p.astype(p_ref.dtype)
            acc_ref[hs, :] = lanes(alpha_ref[hs, :], A_KV_RANK) * acc_ref[hs, :] + _dot(p_ref[hs, :], kv)
        return carry

    lax.fori_loop(0, (i + ATTN_SPAN) // ATTN_SPAN, attn_tile, 0)

    o_lat = acc_ref[...] / jnp.sum(l_ref[...], axis=1, keepdims=True)
    heads = []
    for h in range(A_HEADS):
        heads.append(_dot(o_lat[h * TQ:(h + 1) * TQ, :].astype(wuv_ref.dtype), wuv_ref[h]))
    o = jnp.concatenate(heads, axis=1)
    mix = _dot(o.astype(wout_ref.dtype), wout_ref[...])
    o_ref[0] = _layer_norm(DN_ALPHA * x_ref[0] + mix, g_ref[...], b_ref[...])


def _dsa_attn(x, qlat, qidx, widx, ckv, kidx, w_uv, w_out, ln_g, ln_b):
    bsz, seq, _ = x.shape
    nq = seq // TQ
    top_k = min(TOPK_MAX, seq // 4)
    kern = functools.partial(_dsa_attn_kernel, top_k=top_k)
    rows = A_HEADS * TQ
    return pl.pallas_call(
        kern,
        grid=(bsz, nq),
        in_specs=[
            pl.BlockSpec((1, A_HEADS, TQ, A_KV_RANK), lambda b, i: (b * nq + i, 0, 0, 0)),
            pl.BlockSpec((1, IDX_HEADS, TQ, IDX_PAD), lambda b, i: (b * nq + i, 0, 0, 0)),
            pl.BlockSpec((TQ, LANES), lambda b, i: (b * nq + i, 0)),
            pl.BlockSpec((1, seq, A_KV_RANK), lambda b, i: (b, 0, 0)),
            pl.BlockSpec((1, seq, IDX_PAD), lambda b, i: (b, 0, 0)),
            pl.BlockSpec((1, TQ, D_MODEL), lambda b, i: (b, i, 0)),
            pl.BlockSpec((A_HEADS, A_KV_RANK, A_HEAD_DIM), lambda b, i: (0, 0, 0)),
            pl.BlockSpec((D_MODEL, D_MODEL), lambda b, i: (0, 0)),
            pl.BlockSpec((1, D_MODEL), lambda b, i: (0, 0)),
            pl.BlockSpec((1, D_MODEL), lambda b, i: (0, 0)),
        ],
        out_specs=pl.BlockSpec((1, TQ, D_MODEL), lambda b, i: (b, i, 0)),
        out_shape=jax.ShapeDtypeStruct((bsz, seq, D_MODEL), jnp.float32),
        scratch_shapes=[
            pltpu.VMEM((nq, TQ, TQ), jnp.float32),
            pltpu.VMEM((nq, TQ, TQ), jnp.float32),
            pltpu.VMEM((rows, LANES), jnp.float32),
            pltpu.VMEM((rows, LANES), jnp.float32),
            pltpu.VMEM((rows, LANES), jnp.float32),
            pltpu.VMEM((rows, A_KV_RANK), jnp.float32),
            pltpu.VMEM((rows, ATTN_SPAN * TQ), jnp.float32),
            pltpu.VMEM((rows, ATTN_SPAN * TQ), MXU_DTYPE),
        ],
        compiler_params=pltpu.CompilerParams(
            dimension_semantics=("arbitrary", "arbitrary"), vmem_limit_bytes=VMEM_LIMIT_BYTES),
        name="dsa_attn",
    )(qlat, qidx, widx, ckv.reshape(bsz, seq, A_KV_RANK), kidx.reshape(bsz, seq, IDX_PAD), x,
      w_uv.astype(MXU_DTYPE), w_out.astype(MXU_DTYPE), ln_g[None, :], ln_b[None, :])


def _dsa_layer(x, w_in, g_q, g_kv, w_q_lat, w_q_idx, g_kidx, b_kidx, w_uv, w_out, ln_g, ln_b):
    bsz, seq, _ = x.shape
    qlat, qidx, ckv, kidx, widx = _dsa_proj(x.reshape(bsz * seq, D_MODEL), w_in, g_q, g_kv, w_q_lat, w_q_idx,
                                            g_kidx, b_kidx)
    return _dsa_attn(x, qlat, qidx, widx, ckv, kidx, w_uv, w_out, ln_g, ln_b)


def _hgrn_kernel(x_ref, win_ref, lb_ref, go_ref, wout_ref, g_ref, b_ref, o_ref,
                 st_ref, qp_ref, kp_ref, on_ref):
    t_len = HG_T

    @pl.when(pl.program_id(1) == 0)
    def _():
        st_ref[...] = jnp.zeros_like(st_ref)

    xb = x_ref[0].astype(win_ref.dtype)
    row = lax.broadcasted_iota(jnp.int32, (t_len, B_DIM), 0)
    in_grp = row % SUBLANES
    rr = lax.broadcasted_iota(jnp.int32, (t_len, t_len), 0)
    cc = lax.broadcasted_iota(jnp.int32, (t_len, t_len), 1)
    rc_xor = rr ^ cc

    def head(h, carry):
        p = _dot(xb, win_ref[h])
        q_raw, f_raw, v, g_raw = (p[:, k * B_DIM:(k + 1) * B_DIM] for k in range(4))
        lb = lb_ref[h]
        forget = lb + (1.0 - lb) * _sigmoid(f_raw)
        log_f = jnp.log(forget)
        k = (1.0 - lb) * _sigmoid(-f_raw)
        q = q_raw * _sigmoid(q_raw)

        b = log_f
        sh = 1
        while sh < t_len:
            b = b + jnp.where(row >= sh, pltpu.roll(b, sh, axis=0), 0.0)
            sh *= 2
        b_last = b[t_len - 1:t_len, :]

        st = st_ref[h]
        o = _dot_nt((q * jnp.exp(b)).astype(xb.dtype), st.astype(xb.dtype))
        k_tail = (k * jnp.exp(b_last - b)).astype(xb.dtype)
        st_ref[h] = st * jnp.exp(b_last) + _dot_tn(v.astype(xb.dtype), k_tail)

        prod = jnp.ones_like(forget)
        for j in range(SUBLANES):
            if j == 0:
                kj, vj = k, v
            else:
                kj, vj = pltpu.roll(k, j, axis=0), pltpu.roll(v, j, axis=0)
                prod = prod * (forget if j == 1 else pltpu.roll(forget, j - 1, axis=0))
            term = jnp.where(in_grp >= j, q * kj * prod, 0.0)
            o = o + jnp.sum(term, axis=-1, keepdims=True) * vj

        for li, c in enumerate(HG_LEVELS):
            for m in range(c, t_len, 2 * c):
                ref_b = b[m - 1:m, :]
                qp_ref[li, m:m + c, :] = q[m:m + c, :] * jnp.exp(b[m:m + c, :] - ref_b)
                qp_ref[li, m - c:m, :] = jnp.zeros((c, B_DIM), jnp.float32)
                kp_ref[li, m - c:m, :] = k[m - c:m, :] * jnp.exp(ref_b - b[m - c:m, :])
                kp_ref[li, m:m + c, :] = jnp.zeros((c, B_DIM), jnp.float32)
        a = jnp.zeros((t_len, t_len), jnp.float32)
        for li, c in enumerate(HG_LEVELS):
            al = _dot_nt(qp_ref[li].astype(xb.dtype), kp_ref[li].astype(xb.dtype))
            a = a + jnp.where(rc_xor < 2 * c, al, 0.0)
        o = o + _dot(a.astype(xb.dtype), v.astype(xb.dtype))

        og = o * _sigmoid(g_raw)
        on = og * lax.rsqrt(jnp.mean(og * og, axis=-1, keepdims=True) + RMS_EPS) * go_ref[h]
        on_ref[h] = on.astype(on_ref.dtype)
        return carry

    lax.fori_loop(0, B_HEADS, head, 0)

    on = jnp.concatenate([on_ref[h] for h in range(B_HEADS)], axis=1)
    mix = _dot(on, wout_ref[...])
    o_ref[0] = _layer_norm(DN_ALPHA * x_ref[0] + mix, g_ref[...], b_ref[...])


def _hgrn_layer(x, w_in, lb, g_o, w_out, ln_g, ln_b):
    bsz, seq, _ = x.shape
    win = w_in.reshape(D_MODEL, 4, B_HEADS, B_DIM).transpose(2, 0, 1, 3).reshape(B_HEADS, D_MODEL, 4 * B_DIM)
    n_lv = len(HG_LEVELS)
    return pl.pallas_call(
        _hgrn_kernel,
        grid=(bsz, seq // HG_T),
        in_specs=[
            pl.BlockSpec((1, HG_T, D_MODEL), lambda b, i: (b, i, 0)),
            pl.BlockSpec((B_HEADS, D_MODEL, 4 * B_DIM), lambda b, i: (0, 0, 0)),
            pl.BlockSpec((B_HEADS, 1, B_DIM), lambda b, i: (0, 0, 0)),
            pl.BlockSpec((B_HEADS, 1, B_DIM), lambda b, i: (0, 0, 0)),
            pl.BlockSpec((D_MODEL, D_MODEL), lambda b, i: (0, 0)),
            pl.BlockSpec((1, D_MODEL), lambda b, i: (0, 0)),
            pl.BlockSpec((1, D_MODEL), lambda b, i: (0, 0)),
        ],
        out_specs=pl.BlockSpec((1, HG_T, D_MODEL), lambda b, i: (b, i, 0)),
        out_shape=jax.ShapeDtypeStruct((bsz, seq, D_MODEL), jnp.float32),
        scratch_shapes=[
            pltpu.VMEM((B_HEADS, B_DIM, B_DIM), jnp.float32),
            pltpu.VMEM((n_lv, HG_T, B_DIM), jnp.float32),
            pltpu.VMEM((n_lv, HG_T, B_DIM), jnp.float32),
            pltpu.VMEM((B_HEADS, HG_T, B_DIM), MXU_DTYPE),
        ],
        compiler_params=pltpu.CompilerParams(
            dimension_semantics=("arbitrary", "arbitrary"), vmem_limit_bytes=VMEM_LIMIT_BYTES),
        name="hgrn",
    )(x, win.astype(MXU_DTYPE), lb.reshape(B_HEADS, 1, B_DIM), g_o.reshape(B_HEADS, 1, B_DIM),
      w_out.astype(MXU_DTYPE), ln_g[None, :], ln_b[None, :])


@jax.jit
def kernel(x, a_w_in, a_g_q, a_g_kv, a_w_q_lat, a_w_q_idx, a_g_kidx, a_b_kidx, a_w_uv, a_w_out, b_w_in, b_lb_logits, b_g_o, b_w_out, ln1_g, ln1_b, f_w_up, f_conv_w, f_conv_b, f_w_down, ln2_g, ln2_b):
    bsz, seq, _ = x.shape
    c = jnp.cumsum(jax.nn.softmax(b_lb_logits.astype(jnp.float32), axis=0), axis=0)
    lower_bounds = c - c[0:1]
    for layer in range(DEPTH):
        j = layer // N_MIXERS
        if layer % N_MIXERS == 0:
            x = _dsa_layer(x, a_w_in[j], a_g_q[j], a_g_kv[j], a_w_q_lat[j], a_w_q_idx[j], a_g_kidx[j],
                           a_b_kidx[j], a_w_uv[j], a_w_out[j], ln1_g[layer], ln1_b[layer])
        else:
            x = _hgrn_layer(x, b_w_in[j], lower_bounds[layer], b_g_o[j], b_w_out[j], ln1_g[layer], ln1_b[layer])
        x = _ffn_layer(x.reshape(bsz * seq, D_MODEL), seq, f_w_up[layer], f_conv_w[layer], f_conv_b[layer],
                       f_w_down[layer], ln2_g[layer], ln2_b[layer]).reshape(bsz, seq, D_MODEL)
    return x
```

```python
import functools

import jax
import jax.numpy as jnp
from jax import lax
from jax.experimental import pallas as pl
from jax.experimental.pallas import tpu as pltpu

D_MODEL = 1024
DEPTH = 4
N_MIXERS = 2
A_HEADS = 8
A_HEAD_DIM = D_MODEL // A_HEADS
A_Q_RANK = 384
A_KV_RANK = 256
IDX_HEADS = 8
IDX_DIM = 64
TOPK_MAX = 256
B_HEADS = 8
B_DIM = D_MODEL // B_HEADS
D_FF = 2816
DN_ALPHA = (2 * DEPTH) ** 0.25
LN_EPS = 1e-5
RMS_EPS = 1e-6
LOG2_E = 1.4426950408889634

LANES = 128
SUBLANES = 8
VMEM_LIMIT_BYTES = 56 * 1024 * 1024

MXU_DTYPE = jnp.bfloat16

IDX_PAD = LANES
TQ = 256
ATTN_SPAN = 2
HG_T = 256
HG_LEVELS = (8, 16, 32, 64, 128)
FFN_TM = 512
FFN_HALO = SUBLANES
FFN_CF = 1408

COUNT_ROWS = 4 * SUBLANES
SCORE_MASKED = -3.0e38
BIAS_MASKED = -1.0e30


def _dot(a, b):
    return jnp.dot(a, b, preferred_element_type=jnp.float32)


def _dot_nt(a, b):
    return lax.dot_general(a, b, (((1,), (1,)), ((), ())), preferred_element_type=jnp.float32)


def _dot_tn(a, b):
    return lax.dot_general(a, b, (((0,), (0,)), ((), ())), preferred_element_type=jnp.float32)


def _layer_norm(v, g, b):
    mu = jnp.mean(v, axis=-1, keepdims=True)
    c = v - mu
    var = jnp.mean(c * c, axis=-1, keepdims=True)
    return c * lax.rsqrt(var + LN_EPS) * g + b


def _sigmoid(v):
    return 1.0 / (1.0 + jnp.exp(-v))


def _ffn_kernel(xh_ref, x_ref, wa_ref, wu_ref, cwa_ref, cwu_ref, cba_ref, cbu_ref, wd_ref, g_ref, b_ref,
                o_ref, xb_ref, acc_ref, *, tiles_per_seq):
    i = pl.program_id(0)
    j = pl.program_id(1)

    @pl.when(j == 0)
    def _():
        first = (i % tiles_per_seq) == 0
        halo = jnp.where(first, 0.0, xh_ref[...])
        xb_ref[0:FFN_HALO, :] = halo.astype(xb_ref.dtype)
        xb_ref[FFN_HALO:, :] = x_ref[...].astype(xb_ref.dtype)
        acc_ref[...] = jnp.zeros_like(acc_ref)

    xb = xb_ref[...]

    def conv(h, cw_ref, cb_ref):
        h1 = pltpu.roll(h, 1, axis=0)
        h2 = pltpu.roll(h, 2, axis=0)
        y = h * cw_ref[2:3, :] + h1 * cw_ref[1:2, :] + h2 * cw_ref[0:1, :] + cb_ref[...]
        return y[FFN_HALO:, :]

    a = conv(_dot(xb, wa_ref[...]), cwa_ref, cba_ref)
    u = conv(_dot(xb, wu_ref[...]), cwu_ref, cbu_ref)
    gated = (a * _sigmoid(a)) * u
    acc_ref[...] += _dot(gated.astype(wd_ref.dtype), wd_ref[...])

    @pl.when(j == pl.num_programs(1) - 1)
    def _():
        o_ref[...] = _layer_norm(DN_ALPHA * x_ref[...] + acc_ref[...], g_ref[...], b_ref[...])


def _ffn_layer(x2d, seq, w_up, conv_w, conv_b, w_down, ln_g, ln_b):
    n = x2d.shape[0]
    tm = FFN_TM
    n_ff = D_FF // FFN_CF
    wa = w_up[:, :D_FF].astype(MXU_DTYPE)
    wu = w_up[:, D_FF:].astype(MXU_DTYPE)
    cw = conv_w[:, 0, :]
    cwa, cwu = cw[:, :D_FF], cw[:, D_FF:]
    cba, cbu = conv_b[None, :D_FF], conv_b[None, D_FF:]
    wd = w_down.astype(MXU_DTYPE)
    halo_blocks = tm // FFN_HALO
    kern = functools.partial(_ffn_kernel, tiles_per_seq=seq // tm)
    return pl.pallas_call(
        kern,
        grid=(n // tm, n_ff),
        in_specs=[
            pl.BlockSpec((FFN_HALO, D_MODEL), lambda i, j: (jnp.maximum(i * halo_blocks - 1, 0), 0)),
            pl.BlockSpec((tm, D_MODEL), lambda i, j: (i, 0)),
            pl.BlockSpec((D_MODEL, FFN_CF), lambda i, j: (0, j)),
            pl.BlockSpec((D_MODEL, FFN_CF), lambda i, j: (0, j)),
            pl.BlockSpec((3, FFN_CF), lambda i, j: (0, j)),
            pl.BlockSpec((3, FFN_CF), lambda i, j: (0, j)),
            pl.BlockSpec((1, FFN_CF), lambda i, j: (0, j)),
            pl.BlockSpec((1, FFN_CF), lambda i, j: (0, j)),
            pl.BlockSpec((FFN_CF, D_MODEL), lambda i, j: (j, 0)),
            pl.BlockSpec((1, D_MODEL), lambda i, j: (0, 0)),
            pl.BlockSpec((1, D_MODEL), lambda i, j: (0, 0)),
        ],
        out_specs=pl.BlockSpec((tm, D_MODEL), lambda i, j: (i, 0)),
        out_shape=jax.ShapeDtypeStruct((n, D_MODEL), jnp.float32),
        scratch_shapes=[
            pltpu.VMEM((tm + FFN_HALO, D_MODEL), MXU_DTYPE),
            pltpu.VMEM((tm, D_MODEL), jnp.float32),
        ],
        compiler_params=pltpu.CompilerParams(
            dimension_semantics=("arbitrary", "arbitrary"), vmem_limit_bytes=VMEM_LIMIT_BYTES),
        name="ffn",
    )(x2d, x2d, wa, wu, cwa, cwu, cba, cbu, wd, ln_g[None, :], ln_b[None, :])


_PQ0, _PQ1 = 0, A_Q_RANK
_PKV0, _PKV1 = _PQ1, _PQ1 + A_KV_RANK
_PKI0, _PKI1 = _PKV1, _PKV1 + LANES
_PWI0, _PWI1 = _PKI1, _PKI1 + LANES


def _dsa_proj_kernel(x_ref, wcat_ref, gq_ref, gkv_ref, gk_ref, bk_ref, wql_ref, wqi_ref,
                     qlat_ref, qidx_ref, ckv_ref, kidx_ref, widx_ref):
    xb = x_ref[...].astype(wcat_ref.dtype)
    proj = _dot(xb, wcat_ref[...])
    pq = proj[:, _PQ0:_PQ1]
    cq = pq * lax.rsqrt(jnp.mean(pq * pq, axis=-1, keepdims=True) + RMS_EPS) * gq_ref[...]
    pkv = proj[:, _PKV0:_PKV1]
    ckv = pkv * lax.rsqrt(jnp.mean(pkv * pkv, axis=-1, keepdims=True) + RMS_EPS) * gkv_ref[...]
    ckv_ref[...] = ckv.astype(ckv_ref.dtype)
    pk = proj[:, _PKI0:_PKI1]
    real = lax.broadcasted_iota(jnp.int32, pk.shape, 1) < IDX_DIM
    mu = jnp.sum(pk, axis=-1, keepdims=True) * (1.0 / IDX_DIM)
    c = jnp.where(real, pk - mu, 0.0)
    var = jnp.sum(c * c, axis=-1, keepdims=True) * (1.0 / IDX_DIM)
    kidx = c * lax.rsqrt(var + LN_EPS) * gk_ref[...] + bk_ref[...]
    kidx_ref[...] = kidx.astype(kidx_ref.dtype)
    widx_ref[...] = proj[:, _PWI0:_PWI1] * (IDX_HEADS ** -0.5 * IDX_DIM ** -0.5)
    cqb = cq.astype(wql_ref.dtype)
    ql = _dot(cqb, wql_ref[...]) * (A_KV_RANK ** -0.5 * LOG2_E)
    qi = _dot(cqb, wqi_ref[...])
    for h in range(A_HEADS):
        qlat_ref[0, h] = ql[:, h * A_KV_RANK:(h + 1) * A_KV_RANK].astype(qlat_ref.dtype)
    for h in range(IDX_HEADS):
        qidx_ref[0, h] = qi[:, h * IDX_PAD:(h + 1) * IDX_PAD].astype(qidx_ref.dtype)


def _dsa_proj(x2d, w_in, g_q, g_kv, w_q_lat, w_q_idx, g_kidx, b_kidx):
    n = x2d.shape[0]
    nblk = n // TQ
    o1, o2, o3 = A_Q_RANK, A_Q_RANK + A_KV_RANK, A_Q_RANK + A_KV_RANK + IDX_DIM
    zpad = lambda w, cols: jnp.pad(w, ((0, 0), (0, cols - w.shape[1])))
    wcat = jnp.concatenate(
        [w_in[:, :o2], zpad(w_in[:, o2:o3], LANES), zpad(w_in[:, o3:], LANES)], axis=1).astype(MXU_DTYPE)
    pcols = wcat.shape[1]
    gk = zpad(g_kidx[None, :], LANES)
    bk = zpad(b_kidx[None, :], LANES)
    wqi = jnp.pad(w_q_idx.reshape(A_Q_RANK, IDX_HEADS, IDX_DIM), ((0, 0), (0, 0), (0, IDX_PAD - IDX_DIM)))
    wqi = wqi.reshape(A_Q_RANK, IDX_HEADS * IDX_PAD).astype(MXU_DTYPE)
    wql = w_q_lat.astype(MXU_DTYPE)
    full = lambda shape: pl.BlockSpec(shape, lambda i: (0,) * len(shape))
    return pl.pallas_call(
        _dsa_proj_kernel,
        grid=(nblk,),
        in_specs=[
            pl.BlockSpec((TQ, D_MODEL), lambda i: (i, 0)),
            full((D_MODEL, pcols)), full((1, A_Q_RANK)), full((1, A_KV_RANK)), full((1, LANES)), full((1, LANES)),
            full((A_Q_RANK, A_HEADS * A_KV_RANK)), full((A_Q_RANK, IDX_HEADS * IDX_PAD)),
        ],
        out_specs=[
            pl.BlockSpec((1, A_HEADS, TQ, A_KV_RANK), lambda i: (i, 0, 0, 0)),
            pl.BlockSpec((1, IDX_HEADS, TQ, IDX_PAD), lambda i: (i, 0, 0, 0)),
            pl.BlockSpec((TQ, A_KV_RANK), lambda i: (i, 0)),
            pl.BlockSpec((TQ, IDX_PAD), lambda i: (i, 0)),
            pl.BlockSpec((TQ, LANES), lambda i: (i, 0)),
        ],
        out_shape=[
            jax.ShapeDtypeStruct((nblk, A_HEADS, TQ, A_KV_RANK), MXU_DTYPE),
            jax.ShapeDtypeStruct((nblk, IDX_HEADS, TQ, IDX_PAD), MXU_DTYPE),
            jax.ShapeDtypeStruct((n, A_KV_RANK), MXU_DTYPE),
            jax.ShapeDtypeStruct((n, IDX_PAD), MXU_DTYPE),
            jax.ShapeDtypeStruct((n, LANES), jnp.float32),
        ],
        compiler_params=pltpu.CompilerParams(
            dimension_semantics=("arbitrary",), vmem_limit_bytes=VMEM_LIMIT_BYTES),
        name="dsa_proj",
    )(x2d, wcat, g_q[None, :], g_kv[None, :], gk, bk, wql, wqi)


def _key_to_f32(key):
    bits = key ^ ((key >> 31) & jnp.int32(0x7FFFFFFF))
    return lax.bitcast_convert_type(bits, jnp.float32)


def _dsa_attn_kernel(qlat_ref, qidx_ref, widx_ref, ckv_ref, kidx_ref, x_ref, wuv_ref, wout_ref, g_ref, b_ref,
                     o_ref, sc_ref, sct_ref, m_ref, l_ref, acc_ref, *, top_k):
    i = pl.program_id(1)
    n_tiles = i + 1
    rows = A_HEADS * TQ
    kf = jnp.float32(top_k)
    row_id = lax.broadcasted_iota(jnp.int32, (TQ, TQ), 0)
    col_id = lax.broadcasted_iota(jnp.int32, (TQ, TQ), 1)

    qi = qidx_ref[0].reshape(IDX_HEADS * TQ, IDX_PAD)
    w = widx_ref[...]

    def score_tile(j, carry):
        kt = kidx_ref[0, pl.ds(pl.multiple_of(j * TQ, TQ), TQ), :]
        z = _dot_nt(qi, kt)
        s = jnp.zeros((TQ, TQ), jnp.float32)
        for h in range(IDX_HEADS):
            s = s + w[:, h:h + 1] * jnp.maximum(z[h * TQ:(h + 1) * TQ, :], 0.0)
        valid = (j * TQ + col_id) <= (i * TQ + row_id)
        s = jnp.where(valid, s, SCORE_MASKED)
        sc_ref[j] = s
        sct_ref[j] = s.T
        return carry

    lax.fori_loop(0, n_tiles, score_tile, 0)

    def count(pred):
        def body(j, acc):
            hit = jnp.where(pred(sct_ref[j]), 1.0, 0.0)
            return acc + jnp.sum(hit.reshape(TQ // COUNT_ROWS, COUNT_ROWS, TQ), axis=0)
        acc = lax.fori_loop(0, n_tiles, body, jnp.zeros((COUNT_ROWS, TQ), jnp.float32))
        return jnp.sum(acc, axis=0, keepdims=True)

    n_nonneg = count(lambda s: s >= 0.0)
    key0 = jnp.where(n_nonneg >= kf, jnp.int32(0), jnp.int32(-2 ** 31))

    def bit_step(it, key):
        cand = key | lax.shift_left(jnp.int32(1), 30 - it)
        cand_f = _key_to_f32(cand)
        n_ge = count(lambda s: s >= cand_f)
        return jnp.where(n_ge >= kf, cand, key)

    thr = _key_to_f32(lax.fori_loop(0, 31, bit_step, key0))
    need = jnp.where(thr == SCORE_MASKED, 0.0, kf - count(lambda s: s > thr))

    def to_cols(v):
        return jnp.broadcast_to(v, (LANES, TQ)).T

    thr_c = to_cols(thr)
    need_c = to_cols(need)

    tri = (row_id <= col_id).astype(wuv_ref.dtype)
    thr2 = jnp.concatenate([thr_c, thr_c], axis=1)
    need2 = jnp.concatenate([need_c, need_c], axis=1)

    def bias_tile(j, seen):
        s = sc_ref[j]
        eq = s == thr2
        prefix = _dot(jnp.where(eq, 1.0, 0.0).astype(tri.dtype), tri) + seen
        take = (s > thr2) | (eq & (prefix <= need2))
        sc_ref[j] = jnp.where(take, 0.0, BIAS_MASKED)
        return prefix[:, TQ - 1:TQ]

    lax.fori_loop(0, n_tiles, bias_tile, jnp.zeros((TQ, 1), jnp.float32))

    @pl.when(i % ATTN_SPAN == 0)
    def _():
        sc_ref[i + 1] = jnp.full((TQ, TQ), BIAS_MASKED, jnp.float32)

    q = qlat_ref[0].reshape(rows, A_KV_RANK)
    m_ref[...] = jnp.full(m_ref.shape, BIAS_MASKED, jnp.float32)
    l_ref[...] = jnp.zeros_like(l_ref)
    acc_ref[...] = jnp.zeros_like(acc_ref)
    tk = ATTN_SPAN * TQ

    def lanes(v, width):
        return jnp.concatenate([v] * (width // LANES), axis=1)

    def attn_tile(jj, carry):
        kv = ckv_ref[0, pl.ds(pl.multiple_of(jj * tk, tk), tk), :]
        bias = jnp.concatenate([sc_ref[ATTN_SPAN * jj + u] for u in range(ATTN_SPAN)], axis=1)
        for h in range(A_HEADS):
            hs = slice(h * TQ, (h + 1) * TQ)
            logits = _dot_nt(q[hs], kv) + bias
            m_old = m_ref[hs, :]
            m_new = jnp.maximum(m_old, jnp.max(logits, axis=1, keepdims=True))
            p = jnp.exp2(logits - lanes(m_new, tk))
            alpha = jnp.exp2(m_old - m_new)
            p_lanes = sum(p[:, u * LANES:(u + 1) * LANES] for u in range(tk // LANES))
            l_ref[hs, :] = alpha * l_ref[hs, :] + p_lanes
            m_ref[hs, :] = m_new
            acc_ref[hs, :] = lanes(alpha, A_KV_RANK) * acc_ref[hs, :] + _dot(p.astype(kv.dtype), kv)
        return carry

    lax.fori_loop(0, (i + ATTN_SPAN) // ATTN_SPAN, attn_tile, 0)

    o_lat = acc_ref[...] / jnp.sum(l_ref[...], axis=1, keepdims=True)
    heads = []
    for h in range(A_HEADS):
        heads.append(_dot(o_lat[h * TQ:(h + 1) * TQ, :].astype(wuv_ref.dtype), wuv_ref[h]))
    o = jnp.concatenate(heads, axis=1)
    mix = _dot(o.astype(wout_ref.dtype), wout_ref[...])
    o_ref[0] = _layer_norm(DN_ALPHA * x_ref[0] + mix, g_ref[...], b_ref[...])


def _dsa_attn(x, qlat, qidx, widx, ckv, kidx, w_uv, w_out, ln_g, ln_b):
    bsz, seq, _ = x.shape
    nq = seq // TQ
    top_k = min(TOPK_MAX, seq // 4)
    kern = functools.partial(_dsa_attn_kernel, top_k=top_k)
    rows = A_HEADS * TQ
    return pl.pallas_call(
        kern,
        grid=(bsz, nq),
        in_specs=[
            pl.BlockSpec((1, A_HEADS, TQ, A_KV_RANK), lambda b, i: (b * nq + i, 0, 0, 0)),
            pl.BlockSpec((1, IDX_HEADS, TQ, IDX_PAD), lambda b, i: (b * nq + i, 0, 0, 0)),
            pl.BlockSpec((TQ, LANES), lambda b, i: (b * nq + i, 0)),
            pl.BlockSpec((1, seq, A_KV_RANK), lambda b, i: (b, 0, 0)),
            pl.BlockSpec((1, seq, IDX_PAD), lambda b, i: (b, 0, 0)),
            pl.BlockSpec((1, TQ, D_MODEL), lambda b, i: (b, i, 0)),
            pl.BlockSpec((A_HEADS, A_KV_RANK, A_HEAD_DIM), lambda b, i: (0, 0, 0)),
            pl.BlockSpec((D_MODEL, D_MODEL), lambda b, i: (0, 0)),
            pl.BlockSpec((1, D_MODEL), lambda b, i: (0, 0)),
            pl.BlockSpec((1, D_MODEL), lambda b, i: (0, 0)),
        ],
        out_specs=pl.BlockSpec((1, TQ, D_MODEL), lambda b, i: (b, i, 0)),
        out_shape=jax.ShapeDtypeStruct((bsz, seq, D_MODEL), jnp.float32),
        scratch_shapes=[
            pltpu.VMEM((nq, TQ, TQ), jnp.float32),
            pltpu.VMEM((nq, TQ, TQ), jnp.float32),
            pltpu.VMEM((rows, LANES), jnp.float32),
            pltpu.VMEM((rows, LANES), jnp.float32),
            pltpu.VMEM((rows, A_KV_RANK), jnp.float32),
        ],
        compiler_params=pltpu.CompilerParams(
            dimension_semantics=("arbitrary", "arbitrary"), vmem_limit_bytes=VMEM_LIMIT_BYTES),
        name="dsa_attn",
    )(qlat, qidx, widx, ckv.reshape(bsz, seq, A_KV_RANK), kidx.reshape(bsz, seq, IDX_PAD), x,
      w_uv.astype(MXU_DTYPE), w_out.astype(MXU_DTYPE), ln_g[None, :], ln_b[None, :])


def _dsa_layer(x, w_in, g_q, g_kv, w_q_lat, w_q_idx, g_kidx, b_kidx, w_uv, w_out, ln_g, ln_b):
    bsz, seq, _ = x.shape
    qlat, qidx, ckv, kidx, widx = _dsa_proj(x.reshape(bsz * seq, D_MODEL), w_in, g_q, g_kv, w_q_lat, w_q_idx,
                                            g_kidx, b_kidx)
    return _dsa_attn(x, qlat, qidx, widx, ckv, kidx, w_uv, w_out, ln_g, ln_b)


def _hgrn_kernel(x_ref, win_ref, lb_ref, go_ref, wout_ref, g_ref, b_ref, o_ref,
                 st_ref, qp_ref, kp_ref, on_ref):
    t_len = HG_T

    @pl.when(pl.program_id(1) == 0)
    def _():
        st_ref[...] = jnp.zeros_like(st_ref)

    xb = x_ref[0].astype(win_ref.dtype)
    row = lax.broadcasted_iota(jnp.int32, (t_len, B_DIM), 0)
    in_grp = row % SUBLANES
    rr = lax.broadcasted_iota(jnp.int32, (t_len, t_len), 0)
    cc = lax.broadcasted_iota(jnp.int32, (t_len, t_len), 1)
    rc_xor = rr ^ cc

    def head(h, carry):
        p = _dot(xb, win_ref[h])
        q_raw, f_raw, v, g_raw = (p[:, k * B_DIM:(k + 1) * B_DIM] for k in range(4))
        lb = lb_ref[h]
        forget = lb + (1.0 - lb) * _sigmoid(f_raw)
        log_f = jnp.log(forget)
        k = (1.0 - lb) * _sigmoid(-f_raw)
        q = q_raw * _sigmoid(q_raw)

        b = log_f
        sh = 1
        while sh < t_len:
            b = b + jnp.where(row >= sh, pltpu.roll(b, sh, axis=0), 0.0)
            sh *= 2
        b_last = b[t_len - 1:t_len, :]

        st = st_ref[h]
        o = _dot_nt((q * jnp.exp(b)).astype(xb.dtype), st.astype(xb.dtype))
        k_tail = (k * jnp.exp(b_last - b)).astype(xb.dtype)
        st_ref[h] = st * jnp.exp(b_last) + _dot_tn(v.astype(xb.dtype), k_tail)

        def grp_roll(a, j):
            return pltpu.roll(a.reshape(t_len // SUBLANES, SUBLANES, B_DIM), j, axis=1).reshape(t_len, B_DIM)

        prod = jnp.ones_like(forget)
        for j in range(SUBLANES):
            if j == 0:
                kj, vj = k, v
            else:
                kj, vj = grp_roll(k, j), grp_roll(v, j)
                prod = prod * (forget if j == 1 else grp_roll(forget, j - 1))
            term = jnp.where(in_grp >= j, q * kj * prod, 0.0)
            o = o + jnp.sum(term, axis=-1, keepdims=True) * vj

        for li, c in enumerate(HG_LEVELS):
            for m in range(c, t_len, 2 * c):
                ref_b = b[m - 1:m, :]
                qp_ref[li, m:m + c, :] = q[m:m + c, :] * jnp.exp(b[m:m + c, :] - ref_b)
                qp_ref[li, m - c:m, :] = jnp.zeros((c, B_DIM), jnp.float32)
                kp_ref[li, m - c:m, :] = k[m - c:m, :] * jnp.exp(ref_b - b[m - c:m, :])
                kp_ref[li, m:m + c, :] = jnp.zeros((c, B_DIM), jnp.float32)
        a = None
        for li in reversed(range(len(HG_LEVELS))):
            al = _dot_nt(qp_ref[li].astype(xb.dtype), kp_ref[li].astype(xb.dtype))
            a = al if a is None else jnp.where(rc_xor < 2 * HG_LEVELS[li], al, a)
        o = o + _dot(a.astype(xb.dtype), v.astype(xb.dtype))

        og = o * _sigmoid(g_raw)
        on = og * lax.rsqrt(jnp.mean(og * og, axis=-1, keepdims=True) + RMS_EPS) * go_ref[h]
        on_ref[h] = on.astype(on_ref.dtype)
        return carry

    lax.fori_loop(0, B_HEADS, head, 0, unroll=2)

    on = jnp.concatenate([on_ref[h] for h in range(B_HEADS)], axis=1)
    mix = _dot(on, wout_ref[...])
    o_ref[0] = _layer_norm(DN_ALPHA * x_ref[0] + mix, g_ref[...], b_ref[...])


def _hgrn_layer(x, w_in, lb, g_o, w_out, ln_g, ln_b):
    bsz, seq, _ = x.shape
    win = w_in.reshape(D_MODEL, 4, B_HEADS, B_DIM).transpose(2, 0, 1, 3).reshape(B_HEADS, D_MODEL, 4 * B_DIM)
    n_lv = len(HG_LEVELS)
    return pl.pallas_call(
        _hgrn_kernel,
        grid=(bsz, seq // HG_T),
        in_specs=[
            pl.BlockSpec((1, HG_T, D_MODEL), lambda b, i: (b, i, 0)),
            pl.BlockSpec((B_HEADS, D_MODEL, 4 * B_DIM), lambda b, i: (0, 0, 0)),
            pl.BlockSpec((B_HEADS, 1, B_DIM), lambda b, i: (0, 0, 0)),
            pl.BlockSpec((B_HEADS, 1, B_DIM), lambda b, i: (0, 0, 0)),
            pl.BlockSpec((D_MODEL, D_MODEL), lambda b, i: (0, 0)),
            pl.BlockSpec((1, D_MODEL), lambda b, i: (0, 0)),
            pl.BlockSpec((1, D_MODEL), lambda b, i: (0, 0)),
        ],
        out_specs=pl.BlockSpec((1, HG_T, D_MODEL), lambda b, i: (b, i, 0)),
        out_shape=jax.ShapeDtypeStruct((bsz, seq, D_MODEL), jnp.float32),
        scratch_shapes=[
            pltpu.VMEM((B_HEADS, B_DIM, B_DIM), jnp.float32),
            pltpu.VMEM((n_lv, HG_T, B_DIM), jnp.float32),
            pltpu.VMEM((n_lv, HG_T, B_DIM), jnp.float32),
            pltpu.VMEM((B_HEADS, HG_T, B_DIM), MXU_DTYPE),
        ],
        compiler_params=pltpu.CompilerParams(
            dimension_semantics=("arbitrary", "arbitrary"), vmem_limit_bytes=VMEM_LIMIT_BYTES),
        name="hgrn",
    )(x, win.astype(MXU_DTYPE), lb.reshape(B_HEADS, 1, B_DIM), g_o.reshape(B_HEADS, 1, B_DIM),
      w_out.astype(MXU_DTYPE), ln_g[None, :], ln_b[None, :])


@jax.jit
def kernel(x, a_w_in, a_g_q, a_g_kv, a_w_q_lat, a_w_q_idx, a_g_kidx, a_b_kidx, a_w_uv, a_w_out, b_w_in, b_lb_logits, b_g_o, b_w_out, ln1_g, ln1_b, f_w_up, f_conv_w, f_conv_b, f_w_down, ln2_g, ln2_b):
    bsz, seq, _ = x.shape
    c = jnp.cumsum(jax.nn.softmax(b_lb_logits.astype(jnp.float32), axis=0), axis=0)
    lower_bounds = c - c[0:1]
    for layer in range(DEPTH):
        j = layer // N_MIXERS
        if layer % N_MIXERS == 0:
            x = _dsa_layer(x, a_w_in[j], a_g_q[j], a_g_kv[j], a_w_q_lat[j], a_w_q_idx[j], a_g_kidx[j],
                           a_b_kidx[j], a_w_uv[j], a_w_out[j], ln1_g[layer], ln1_b[layer])
        else:
            x = _hgrn_layer(x, b_w_in[j], lower_bounds[layer], b_g_o[j], b_w_out[j], ln1_g[layer], ln1_b[layer])
        x = _ffn_layer(x.reshape(bsz * seq, D_MODEL), seq, f_w_up[layer], f_conv_w[layer], f_conv_b[layer],
                       f_w_down[layer], ln2_g[layer], ln2_b[layer]).reshape(bsz, seq, D_MODEL)
    return x
```

```python
import functools

import jax
import jax.numpy as jnp
from jax import lax
from jax.experimental import pallas as pl
from jax.experimental.pallas import tpu as pltpu

D_MODEL = 1024
DEPTH = 4
N_MIXERS = 2
A_HEADS = 8
A_HEAD_DIM = D_MODEL // A_HEADS
A_Q_RANK = 384
A_KV_RANK = 256
IDX_HEADS = 8
IDX_DIM = 64
TOPK_MAX = 256
B_HEADS = 8
B_DIM = D_MODEL // B_HEADS
D_FF = 2816
DN_ALPHA = (2 * DEPTH) ** 0.25
LN_EPS = 1e-5
RMS_EPS = 1e-6
LOG2_E = 1.4426950408889634

LANES = 128
SUBLANES = 8
VMEM_LIMIT_BYTES = 56 * 1024 * 1024

MXU_DTYPE = jnp.bfloat16

IDX_PAD = LANES
TQ = 256
ATTN_SPAN = 2
HG_T = 256
HG_LEVELS = (8, 16, 32, 64, 128)
FFN_TM = 512
FFN_HALO = SUBLANES
FFN_SLAB = 256

COUNT_ROWS = 4 * SUBLANES
SCORE_MASKED = -3.0e38
BIAS_MASKED = -1.0e30


def _dot(a, b):
    return jnp.dot(a, b, preferred_element_type=jnp.float32)


def _dot_nt(a, b):
    return lax.dot_general(a, b, (((1,), (1,)), ((), ())), preferred_element_type=jnp.float32)


def _dot_tn(a, b):
    return lax.dot_general(a, b, (((0,), (0,)), ((), ())), preferred_element_type=jnp.float32)


def _layer_norm(v, g, b):
    mu = jnp.mean(v, axis=-1, keepdims=True)
    c = v - mu
    var = jnp.mean(c * c, axis=-1, keepdims=True)
    return c * lax.rsqrt(var + LN_EPS) * g + b


def _sigmoid(v):
    return 1.0 / (1.0 + jnp.exp(-v))


def _ffn_kernel(xh_ref, x_ref, wa_ref, wu_ref, cwa_ref, cwu_ref, cba_ref, cbu_ref, wd_ref, g_ref, b_ref,
                o_ref, hid_ref, *, tiles_per_seq):
    i = pl.program_id(0)
    first = (i % tiles_per_seq) == 0
    halo = jnp.where(first, 0.0, xh_ref[...])
    xb = jnp.concatenate([halo, x_ref[...]], axis=0).astype(wa_ref.dtype)

    def conv(h, cw, cb):
        h1 = pltpu.roll(h, 1, axis=0)
        h2 = pltpu.roll(h, 2, axis=0)
        y = h * cw[2:3, :] + h1 * cw[1:2, :] + h2 * cw[0:1, :] + cb
        return y[FFN_HALO:, :]

    for c in range(D_FF // FFN_SLAB):
        cs = slice(c * FFN_SLAB, (c + 1) * FFN_SLAB)
        a = conv(_dot(xb, wa_ref[:, cs]), cwa_ref[:, cs], cba_ref[:, cs])
        u = conv(_dot(xb, wu_ref[:, cs]), cwu_ref[:, cs], cbu_ref[:, cs])
        hid_ref[:, cs] = ((a * _sigmoid(a)) * u).astype(hid_ref.dtype)
    mix = _dot(hid_ref[...], wd_ref[...])
    o_ref[...] = _layer_norm(DN_ALPHA * x_ref[...] + mix, g_ref[...], b_ref[...])


def _ffn_layer(x2d, seq, w_up, conv_w, conv_b, w_down, ln_g, ln_b):
    n = x2d.shape[0]
    tm = FFN_TM
    wa = w_up[:, :D_FF].astype(MXU_DTYPE)
    wu = w_up[:, D_FF:].astype(MXU_DTYPE)
    cw = conv_w[:, 0, :]
    cwa, cwu = cw[:, :D_FF], cw[:, D_FF:]
    cba, cbu = conv_b[None, :D_FF], conv_b[None, D_FF:]
    wd = w_down.astype(MXU_DTYPE)
    halo_blocks = tm // FFN_HALO
    kern = functools.partial(_ffn_kernel, tiles_per_seq=seq // tm)
    whole = lambda shape: pl.BlockSpec(shape, lambda i: (0,) * len(shape), pipeline_mode=pl.Buffered(1))
    return pl.pallas_call(
        kern,
        grid=(n // tm,),
        in_specs=[
            pl.BlockSpec((FFN_HALO, D_MODEL), lambda i: (jnp.maximum(i * halo_blocks - 1, 0), 0)),
            pl.BlockSpec((tm, D_MODEL), lambda i: (i, 0)),
            whole((D_MODEL, D_FF)), whole((D_MODEL, D_FF)),
            whole((3, D_FF)), whole((3, D_FF)), whole((1, D_FF)), whole((1, D_FF)),
            whole((D_FF, D_MODEL)), whole((1, D_MODEL)), whole((1, D_MODEL)),
        ],
        out_specs=pl.BlockSpec((tm, D_MODEL), lambda i: (i, 0)),
        out_shape=jax.ShapeDtypeStruct((n, D_MODEL), jnp.float32),
        scratch_shapes=[pltpu.VMEM((tm, D_FF), MXU_DTYPE)],
        compiler_params=pltpu.CompilerParams(
            dimension_semantics=("arbitrary",), vmem_limit_bytes=VMEM_LIMIT_BYTES),
        name="ffn",
    )(x2d, x2d, wa, wu, cwa, cwu, cba, cbu, wd, ln_g[None, :], ln_b[None, :])


_PQ0, _PQ1 = 0, A_Q_RANK
_PKV0, _PKV1 = _PQ1, _PQ1 + A_KV_RANK
_PKI0, _PKI1 = _PKV1, _PKV1 + LANES
_PWI0, _PWI1 = _PKI1, _PKI1 + LANES


def _dsa_proj_kernel(x_ref, wcat_ref, gq_ref, gkv_ref, gk_ref, bk_ref, wql_ref, wqi_ref,
                     qlat_ref, qidx_ref, ckv_ref, kidx_ref, widx_ref):
    xb = x_ref[...].astype(wcat_ref.dtype)
    proj = _dot(xb, wcat_ref[...])
    pq = proj[:, _PQ0:_PQ1]
    cq = pq * lax.rsqrt(jnp.mean(pq * pq, axis=-1, keepdims=True) + RMS_EPS) * gq_ref[...]
    pkv = proj[:, _PKV0:_PKV1]
    ckv = pkv * lax.rsqrt(jnp.mean(pkv * pkv, axis=-1, keepdims=True) + RMS_EPS) * gkv_ref[...]
    ckv_ref[...] = ckv.astype(ckv_ref.dtype)
    pk = proj[:, _PKI0:_PKI1]
    real = lax.broadcasted_iota(jnp.int32, pk.shape, 1) < IDX_DIM
    mu = jnp.sum(pk, axis=-1, keepdims=True) * (1.0 / IDX_DIM)
    c = jnp.where(real, pk - mu, 0.0)
    var = jnp.sum(c * c, axis=-1, keepdims=True) * (1.0 / IDX_DIM)
    kidx = c * lax.rsqrt(var + LN_EPS) * gk_ref[...] + bk_ref[...]
    kidx_ref[...] = kidx.astype(kidx_ref.dtype)
    widx_ref[...] = proj[:, _PWI0:_PWI1] * (IDX_HEADS ** -0.5 * IDX_DIM ** -0.5)
    cqb = cq.astype(wql_ref.dtype)
    ql = _dot(cqb, wql_ref[...]) * (A_KV_RANK ** -0.5 * LOG2_E)
    qi = _dot(cqb, wqi_ref[...])
    for h in range(A_HEADS):
        qlat_ref[0, h] = ql[:, h * A_KV_RANK:(h + 1) * A_KV_RANK].astype(qlat_ref.dtype)
    for h in range(IDX_HEADS):
        qidx_ref[0, h] = qi[:, h * IDX_PAD:(h + 1) * IDX_PAD].astype(qidx_ref.dtype)


def _dsa_proj(x2d, w_in, g_q, g_kv, w_q_lat, w_q_idx, g_kidx, b_kidx):
    n = x2d.shape[0]
    nblk = n // TQ
    o1, o2, o3 = A_Q_RANK, A_Q_RANK + A_KV_RANK, A_Q_RANK + A_KV_RANK + IDX_DIM
    zpad = lambda w, cols: jnp.pad(w, ((0, 0), (0, cols - w.shape[1])))
    wcat = jnp.concatenate(
        [w_in[:, :o2], zpad(w_in[:, o2:o3], LANES), zpad(w_in[:, o3:], LANES)], axis=1).astype(MXU_DTYPE)
    pcols = wcat.shape[1]
    gk = zpad(g_kidx[None, :], LANES)
    bk = zpad(b_kidx[None, :], LANES)
    wqi = jnp.pad(w_q_idx.reshape(A_Q_RANK, IDX_HEADS, IDX_DIM), ((0, 0), (0, 0), (0, IDX_PAD - IDX_DIM)))
    wqi = wqi.reshape(A_Q_RANK, IDX_HEADS * IDX_PAD).astype(MXU_DTYPE)
    wql = w_q_lat.astype(MXU_DTYPE)
    full = lambda shape: pl.BlockSpec(shape, lambda i: (0,) * len(shape))
    return pl.pallas_call(
        _dsa_proj_kernel,
        grid=(nblk,),
        in_specs=[
            pl.BlockSpec((TQ, D_MODEL), lambda i: (i, 0)),
            full((D_MODEL, pcols)), full((1, A_Q_RANK)), full((1, A_KV_RANK)), full((1, LANES)), full((1, LANES)),
            full((A_Q_RANK, A_HEADS * A_KV_RANK)), full((A_Q_RANK, IDX_HEADS * IDX_PAD)),
        ],
        out_specs=[
            pl.BlockSpec((1, A_HEADS, TQ, A_KV_RANK), lambda i: (i, 0, 0, 0)),
            pl.BlockSpec((1, IDX_HEADS, TQ, IDX_PAD), lambda i: (i, 0, 0, 0)),
            pl.BlockSpec((TQ, A_KV_RANK), lambda i: (i, 0)),
            pl.BlockSpec((TQ, IDX_PAD), lambda i: (i, 0)),
            pl.BlockSpec((TQ, LANES), lambda i: (i, 0)),
        ],
        out_shape=[
            jax.ShapeDtypeStruct((nblk, A_HEADS, TQ, A_KV_RANK), MXU_DTYPE),
            jax.ShapeDtypeStruct((nblk, IDX_HEADS, TQ, IDX_PAD), MXU_DTYPE),
            jax.ShapeDtypeStruct((n, A_KV_RANK), MXU_DTYPE),
            jax.ShapeDtypeStruct((n, IDX_PAD), MXU_DTYPE),
            jax.ShapeDtypeStruct((n, LANES), jnp.float32),
        ],
        compiler_params=pltpu.CompilerParams(
            dimension_semantics=("arbitrary",), vmem_limit_bytes=VMEM_LIMIT_BYTES),
        name="dsa_proj",
    )(x2d, wcat, g_q[None, :], g_kv[None, :], gk, bk, wql, wqi)


def _key_to_f32(key):
    bits = key ^ ((key >> 31) & jnp.int32(0x7FFFFFFF))
    return lax.bitcast_convert_type(bits, jnp.float32)


def _dsa_attn_kernel(qlat_ref, qidx_ref, widx_ref, ckv_ref, kidx_ref, x_ref, wuv_ref, wout_ref, g_ref, b_ref,
                     o_ref, sc_ref, sct_ref, m_ref, l_ref, acc_ref, *, top_k):
    i = pl.program_id(1)
    n_tiles = i + 1
    rows = A_HEADS * TQ
    kf = jnp.float32(top_k)
    row_id = lax.broadcasted_iota(jnp.int32, (TQ, TQ), 0)
    col_id = lax.broadcasted_iota(jnp.int32, (TQ, TQ), 1)

    qi = qidx_ref[0].reshape(IDX_HEADS * TQ, IDX_PAD)
    w = widx_ref[...]

    def score_tile(j, carry):
        kt = kidx_ref[0, pl.ds(pl.multiple_of(j * TQ, TQ), TQ), :]
        z = _dot_nt(qi, kt)
        s = jnp.zeros((TQ, TQ), jnp.float32)
        for h in range(IDX_HEADS):
            s = s + w[:, h:h + 1] * jnp.maximum(z[h * TQ:(h + 1) * TQ, :], 0.0)
        valid = (j * TQ + col_id) <= (i * TQ + row_id)
        s = jnp.where(valid, s, SCORE_MASKED)
        sc_ref[j] = s
        sct_ref[j] = s.T
        return carry

    lax.fori_loop(0, n_tiles, score_tile, 0)

    def count(pred):
        def body(j, acc):
            hit = jnp.where(pred(sct_ref[j]), 1.0, 0.0)
            return acc + jnp.sum(hit.reshape(TQ // COUNT_ROWS, COUNT_ROWS, TQ), axis=0)
        acc = lax.fori_loop(0, n_tiles, body, jnp.zeros((COUNT_ROWS, TQ), jnp.float32))
        return jnp.sum(acc, axis=0, keepdims=True)

    n_nonneg = count(lambda s: s >= 0.0)
    key0 = jnp.where(n_nonneg >= kf, jnp.int32(0), jnp.int32(-2 ** 31))

    def bit_step(it, key):
        cand = key | lax.shift_left(jnp.int32(1), 30 - it)
        cand_f = _key_to_f32(cand)
        n_ge = count(lambda s: s >= cand_f)
        return jnp.where(n_ge >= kf, cand, key)

    thr = _key_to_f32(lax.fori_loop(0, 31, bit_step, key0))
    need = jnp.where(thr == SCORE_MASKED, 0.0, kf - count(lambda s: s > thr))

    def to_cols(v):
        return jnp.broadcast_to(v, (LANES, TQ)).T

    thr_c = to_cols(thr)
    need_c = to_cols(need)

    tri = (row_id <= col_id).astype(wuv_ref.dtype)
    thr2 = jnp.concatenate([thr_c, thr_c], axis=1)
    need2 = jnp.concatenate([need_c, need_c], axis=1)

    def bias_tile(j, seen):
        s = sc_ref[j]
        eq = s == thr2
        prefix = _dot(jnp.where(eq, 1.0, 0.0).astype(tri.dtype), tri) + seen
        take = (s > thr2) | (eq & (prefix <= need2))
        sc_ref[j] = jnp.where(take, 0.0, BIAS_MASKED)
        return prefix[:, TQ - 1:TQ]

    lax.fori_loop(0, n_tiles, bias_tile, jnp.zeros((TQ, 1), jnp.float32))

    @pl.when(i % ATTN_SPAN == 0)
    def _():
        sc_ref[i + 1] = jnp.full((TQ, TQ), BIAS_MASKED, jnp.float32)

    q = qlat_ref[0].reshape(rows, A_KV_RANK)
    m_ref[...] = jnp.full(m_ref.shape, BIAS_MASKED, jnp.float32)
    l_ref[...] = jnp.zeros_like(l_ref)
    acc_ref[...] = jnp.zeros_like(acc_ref)
    tk = ATTN_SPAN * TQ

    def lanes(v, width):
        return jnp.concatenate([v] * (width // LANES), axis=1)

    def attn_tile(jj, carry):
        kv = ckv_ref[0, pl.ds(pl.multiple_of(jj * tk, tk), tk), :]
        bias = jnp.concatenate([sc_ref[ATTN_SPAN * jj + u] for u in range(ATTN_SPAN)], axis=1)
        for h in range(A_HEADS):
            hs = slice(h * TQ, (h + 1) * TQ)
            logits = _dot_nt(q[hs], kv) + bias
            m_old = m_ref[hs, :]
            m_new = jnp.maximum(m_old, jnp.max(logits, axis=1, keepdims=True))
            p = jnp.exp2(logits - lanes(m_new, tk))
            alpha = jnp.exp2(m_old - m_new)
            p_lanes = sum(p[:, u * LANES:(u + 1) * LANES] for u in range(tk // LANES))
            l_ref[hs, :] = alpha * l_ref[hs, :] + p_lanes
            m_ref[hs, :] = m_new
            acc_ref[hs, :] = lanes(alpha, A_KV_RANK) * acc_ref[hs, :] + _dot(p.astype(kv.dtype), kv)
        return carry

    lax.fori_loop(0, (i + ATTN_SPAN) // ATTN_SPAN, attn_tile, 0)

    o_lat = acc_ref[...] / jnp.sum(l_ref[...], axis=1, keepdims=True)
    heads = []
    for h in range(A_HEADS):
        heads.append(_dot(o_lat[h * TQ:(h + 1) * TQ, :].astype(wuv_ref.dtype), wuv_ref[h]))
    o = jnp.concatenate(heads, axis=1)
    mix = _dot(o.astype(wout_ref.dtype), wout_ref[...])
    o_ref[0] = _layer_norm(DN_ALPHA * x_ref[0] + mix, g_ref[...], b_ref[...])


def _dsa_attn(x, qlat, qidx, widx, ckv, kidx, w_uv, w_out, ln_g, ln_b):
    bsz, seq, _ = x.shape
    nq = seq // TQ
    top_k = min(TOPK_MAX, seq // 4)
    kern = functools.partial(_dsa_attn_kernel, top_k=top_k)
    rows = A_HEADS * TQ
    return pl.pallas_call(
        kern,
        grid=(bsz, nq),
        in_specs=[
            pl.BlockSpec((1, A_HEADS, TQ, A_KV_RANK), lambda b, i: (b * nq + i, 0, 0, 0)),
            pl.BlockSpec((1, IDX_HEADS, TQ, IDX_PAD), lambda b, i: (b * nq + i, 0, 0, 0)),
            pl.BlockSpec((TQ, LANES), lambda b, i: (b * nq + i, 0)),
            pl.BlockSpec((1, seq, A_KV_RANK), lambda b, i: (b, 0, 0)),
            pl.BlockSpec((1, seq, IDX_PAD), lambda b, i: (b, 0, 0)),
            pl.BlockSpec((1, TQ, D_MODEL), lambda b, i: (b, i, 0)),
            pl.BlockSpec((A_HEADS, A_KV_RANK, A_HEAD_DIM), lambda b, i: (0, 0, 0)),
            pl.BlockSpec((D_MODEL, D_MODEL), lambda b, i: (0, 0)),
            pl.BlockSpec((1, D_MODEL), lambda b, i: (0, 0)),
            pl.BlockSpec((1, D_MODEL), lambda b, i: (0, 0)),
        ],
        out_specs=pl.BlockSpec((1, TQ, D_MODEL), lambda b, i: (b, i, 0)),
        out_shape=jax.ShapeDtypeStruct((bsz, seq, D_MODEL), jnp.float32),
        scratch_shapes=[
            pltpu.VMEM((nq, TQ, TQ), jnp.float32),
            pltpu.VMEM((nq, TQ, TQ), jnp.float32),
            pltpu.VMEM((rows, LANES), jnp.float32),
            pltpu.VMEM((rows, LANES), jnp.float32),
            pltpu.VMEM((rows, A_KV_RANK), jnp.float32),
        ],
        compiler_params=pltpu.CompilerParams(
            dimension_semantics=("arbitrary", "arbitrary"), vmem_limit_bytes=VMEM_LIMIT_BYTES),
        name="dsa_attn",
    )(qlat, qidx, widx, ckv.reshape(bsz, seq, A_KV_RANK), kidx.reshape(bsz, seq, IDX_PAD), x,
      w_uv.astype(MXU_DTYPE), w_out.astype(MXU_DTYPE), ln_g[None, :], ln_b[None, :])


def _dsa_layer(x, w_in, g_q, g_kv, w_q_lat, w_q_idx, g_kidx, b_kidx, w_uv, w_out, ln_g, ln_b):
    bsz, seq, _ = x.shape
    qlat, qidx, ckv, kidx, widx = _dsa_proj(x.reshape(bsz * seq, D_MODEL), w_in, g_q, g_kv, w_q_lat, w_q_idx,
                                            g_kidx, b_kidx)
    return _dsa_attn(x, qlat, qidx, widx, ckv, kidx, w_uv, w_out, ln_g, ln_b)


def _hgrn_kernel(x_ref, win_ref, lb_ref, go_ref, wout_ref, g_ref, b_ref, o_ref,
                 st_ref, qp_ref, kp_ref, on_ref):
    t_len = HG_T

    @pl.when(pl.program_id(1) == 0)
    def _():
        st_ref[...] = jnp.zeros_like(st_ref)

    xb = x_ref[0].astype(win_ref.dtype)
    row = lax.broadcasted_iota(jnp.int32, (t_len, B_DIM), 0)
    in_grp = row % SUBLANES
    rr = lax.broadcasted_iota(jnp.int32, (t_len, t_len), 0)
    cc = lax.broadcasted_iota(jnp.int32, (t_len, t_len), 1)
    rc_xor = rr ^ cc

    def head(h, carry):
        p = _dot(xb, win_ref[h])
        q_raw, f_raw, v, g_raw = (p[:, k * B_DIM:(k + 1) * B_DIM] for k in range(4))
        lb = lb_ref[h]
        forget = lb + (1.0 - lb) * _sigmoid(f_raw)
        log_f = jnp.log(forget)
        k = (1.0 - lb) * _sigmoid(-f_raw)
        q = q_raw * _sigmoid(q_raw)

        b = log_f
        sh = 1
        while sh < t_len:
            b = b + jnp.where(row >= sh, pltpu.roll(b, sh, axis=0), 0.0)
            sh *= 2
        b_last = b[t_len - 1:t_len, :]

        st = st_ref[h]
        o = _dot_nt((q * jnp.exp(b)).astype(xb.dtype), st.astype(xb.dtype))
        k_tail = (k * jnp.exp(b_last - b)).astype(xb.dtype)
        st_ref[h] = st * jnp.exp(b_last) + _dot_tn(v.astype(xb.dtype), k_tail)

        def grp_roll(a, j):
            return pltpu.roll(a.reshape(t_len // SUBLANES, SUBLANES, B_DIM), j, axis=1).reshape(t_len, B_DIM)

        prod = jnp.ones_like(forget)
        for j in range(SUBLANES):
            if j == 0:
                kj, vj = k, v
            else:
                kj, vj = grp_roll(k, j), grp_roll(v, j)
                prod = prod * (forget if j == 1 else grp_roll(forget, j - 1))
            term = jnp.where(in_grp >= j, q * kj * prod, 0.0)
            o = o + jnp.sum(term, axis=-1, keepdims=True) * vj

        for li, c in enumerate(HG_LEVELS):
            for m in range(c, t_len, 2 * c):
                ref_b = b[m - 1:m, :]
                qp_ref[li, m:m + c, :] = q[m:m + c, :] * jnp.exp(b[m:m + c, :] - ref_b)
                qp_ref[li, m - c:m, :] = jnp.zeros((c, B_DIM), jnp.float32)
                kp_ref[li, m - c:m, :] = k[m - c:m, :] * jnp.exp(ref_b - b[m - c:m, :])
                kp_ref[li, m:m + c, :] = jnp.zeros((c, B_DIM), jnp.float32)
        a = None
        for li in reversed(range(len(HG_LEVELS))):
            al = _dot_nt(qp_ref[li].astype(xb.dtype), kp_ref[li].astype(xb.dtype))
            a = al if a is None else jnp.where(rc_xor < 2 * HG_LEVELS[li], al, a)
        o = o + _dot(a.astype(xb.dtype), v.astype(xb.dtype))

        og = o * _sigmoid(g_raw)
        on = og * lax.rsqrt(jnp.mean(og * og, axis=-1, keepdims=True) + RMS_EPS) * go_ref[h]
        on_ref[h] = on.astype(on_ref.dtype)
        return carry

    lax.fori_loop(0, B_HEADS, head, 0, unroll=2)

    on = jnp.concatenate([on_ref[h] for h in range(B_HEADS)], axis=1)
    mix = _dot(on, wout_ref[...])
    o_ref[0] = _layer_norm(DN_ALPHA * x_ref[0] + mix, g_ref[...], b_ref[...])


def _hgrn_layer(x, w_in, lb, g_o, w_out, ln_g, ln_b):
    bsz, seq, _ = x.shape
    win = w_in.reshape(D_MODEL, 4, B_HEADS, B_DIM).transpose(2, 0, 1, 3).reshape(B_HEADS, D_MODEL, 4 * B_DIM)
    n_lv = len(HG_LEVELS)
    return pl.pallas_call(
        _hgrn_kernel,
        grid=(bsz, seq // HG_T),
        in_specs=[
            pl.BlockSpec((1, HG_T, D_MODEL), lambda b, i: (b, i, 0)),
            pl.BlockSpec((B_HEADS, D_MODEL, 4 * B_DIM), lambda b, i: (0, 0, 0)),
            pl.BlockSpec((B_HEADS, 1, B_DIM), lambda b, i: (0, 0, 0)),
            pl.BlockSpec((B_HEADS, 1, B_DIM), lambda b, i: (0, 0, 0)),
            pl.BlockSpec((D_MODEL, D_MODEL), lambda b, i: (0, 0)),
            pl.BlockSpec((1, D_MODEL), lambda b, i: (0, 0)),
            pl.BlockSpec((1, D_MODEL), lambda b, i: (0, 0)),
        ],
        out_specs=pl.BlockSpec((1, HG_T, D_MODEL), lambda b, i: (b, i, 0)),
        out_shape=jax.ShapeDtypeStruct((bsz, seq, D_MODEL), jnp.float32),
        scratch_shapes=[
            pltpu.VMEM((B_HEADS, B_DIM, B_DIM), jnp.float32),
            pltpu.VMEM((n_lv, HG_T, B_DIM), jnp.float32),
            pltpu.VMEM((n_lv, HG_T, B_DIM), jnp.float32),
            pltpu.VMEM((B_HEADS, HG_T, B_DIM), MXU_DTYPE),
        ],
        compiler_params=pltpu.CompilerParams(
            dimension_semantics=("arbitrary", "arbitrary"), vmem_limit_bytes=VMEM_LIMIT_BYTES),
        name="hgrn",
    )(x, win.astype(MXU_DTYPE), lb.reshape(B_HEADS, 1, B_DIM), g_o.reshape(B_HEADS, 1, B_DIM),
      w_out.astype(MXU_DTYPE), ln_g[None, :], ln_b[None, :])


@jax.jit
def kernel(x, a_w_in, a_g_q, a_g_kv, a_w_q_lat, a_w_q_idx, a_g_kidx, a_b_kidx, a_w_uv, a_w_out, b_w_in, b_lb_logits, b_g_o, b_w_out, ln1_g, ln1_b, f_w_up, f_conv_w, f_conv_b, f_w_down, ln2_g, ln2_b):
    bsz, seq, _ = x.shape
    c = jnp.cumsum(jax.nn.softmax(b_lb_logits.astype(jnp.float32), axis=0), axis=0)
    lower_bounds = c - c[0:1]
    for layer in range(DEPTH):
        j = layer // N_MIXERS
        if layer % N_MIXERS == 0:
            x = _dsa_layer(x, a_w_in[j], a_g_q[j], a_g_kv[j], a_w_q_lat[j], a_w_q_idx[j], a_g_kidx[j],
                           a_b_kidx[j], a_w_uv[j], a_w_out[j], ln1_g[layer], ln1_b[layer])
        else:
            x = _hgrn_layer(x, b_w_in[j], lower_bounds[layer], b_g_o[j], b_w_out[j], ln1_g[layer], ln1_b[layer])
        x = _ffn_layer(x.reshape(bsz * seq, D_MODEL), seq, f_w_up[layer], f_conv_w[layer], f_conv_b[layer],
                       f_w_down[layer], ln2_g[layer], ln2_b[layer]).reshape(bsz, seq, D_MODEL)
    return x
```

```python
import functools

import jax
import jax.numpy as jnp
from jax import lax
from jax.experimental import pallas as pl
from jax.experimental.pallas import tpu as pltpu

D_MODEL = 1024
DEPTH = 4
N_MIXERS = 2
A_HEADS = 8
A_HEAD_DIM = D_MODEL // A_HEADS
A_Q_RANK = 384
A_KV_RANK = 256
IDX_HEADS = 8
IDX_DIM = 64
TOPK_MAX = 256
B_HEADS = 8
B_DIM = D_MODEL // B_HEADS
D_FF = 2816
DN_ALPHA = (2 * DEPTH) ** 0.25
LN_EPS = 1e-5
RMS_EPS = 1e-6
LOG2_E = 1.4426950408889634

LANES = 128
SUBLANES = 8
VMEM_LIMIT_BYTES = 56 * 1024 * 1024

MXU_DTYPE = jnp.bfloat16

IDX_PAD = LANES
TQ = 256
ATTN_SPAN = 2
HG_T = 256
HG_LEVELS = (8, 16, 32, 64, 128)
FFN_TM = 512
FFN_HALO = SUBLANES
FFN_SLAB = 256

SEARCH_STEPS = 31
SEARCH_FIXED_STEPS = 24
COUNT_ROWS = 4 * SUBLANES
SCORE_MASKED = -3.0e38
BIAS_MASKED = -1.0e30


def _dot(a, b):
    return jnp.dot(a, b, preferred_element_type=jnp.float32)


def _dot_nt(a, b):
    return lax.dot_general(a, b, (((1,), (1,)), ((), ())), preferred_element_type=jnp.float32)


def _dot_tn(a, b):
    return lax.dot_general(a, b, (((0,), (0,)), ((), ())), preferred_element_type=jnp.float32)


def _layer_norm(v, g, b):
    mu = jnp.mean(v, axis=-1, keepdims=True)
    c = v - mu
    var = jnp.mean(c * c, axis=-1, keepdims=True)
    return c * lax.rsqrt(var + LN_EPS) * g + b


def _sigmoid(v):
    return 1.0 / (1.0 + jnp.exp(-v))


def _ffn_kernel(xh_ref, x_ref, wa_ref, wu_ref, cwa_ref, cwu_ref, cba_ref, cbu_ref, wd_ref, g_ref, b_ref,
                o_ref, hid_ref, *, tiles_per_seq):
    i = pl.program_id(0)
    first = (i % tiles_per_seq) == 0
    halo = jnp.where(first, 0.0, xh_ref[...])
    xb = jnp.concatenate([halo, x_ref[...]], axis=0).astype(wa_ref.dtype)

    def conv(h, cw, cb):
        h1 = pltpu.roll(h, 1, axis=0)
        h2 = pltpu.roll(h, 2, axis=0)
        y = h * cw[2:3, :] + h1 * cw[1:2, :] + h2 * cw[0:1, :] + cb
        return y[FFN_HALO:, :]

    for c in range(D_FF // FFN_SLAB):
        cs = slice(c * FFN_SLAB, (c + 1) * FFN_SLAB)
        a = conv(_dot(xb, wa_ref[:, cs]), cwa_ref[:, cs], cba_ref[:, cs])
        u = conv(_dot(xb, wu_ref[:, cs]), cwu_ref[:, cs], cbu_ref[:, cs])
        hid_ref[:, cs] = ((a * _sigmoid(a)) * u).astype(hid_ref.dtype)
    mix = _dot(hid_ref[...], wd_ref[...])
    o_ref[...] = _layer_norm(DN_ALPHA * x_ref[...] + mix, g_ref[...], b_ref[...])


def _ffn_layer(x2d, seq, w_up, conv_w, conv_b, w_down, ln_g, ln_b):
    n = x2d.shape[0]
    tm = FFN_TM
    wa = w_up[:, :D_FF].astype(MXU_DTYPE)
    wu = w_up[:, D_FF:].astype(MXU_DTYPE)
    cw = conv_w[:, 0, :]
    cwa, cwu = cw[:, :D_FF], cw[:, D_FF:]
    cba, cbu = conv_b[None, :D_FF], conv_b[None, D_FF:]
    wd = w_down.astype(MXU_DTYPE)
    halo_blocks = tm // FFN_HALO
    kern = functools.partial(_ffn_kernel, tiles_per_seq=seq // tm)
    whole = lambda shape: pl.BlockSpec(shape, lambda i: (0,) * len(shape), pipeline_mode=pl.Buffered(1))
    return pl.pallas_call(
        kern,
        grid=(n // tm,),
        in_specs=[
            pl.BlockSpec((FFN_HALO, D_MODEL), lambda i: (jnp.maximum(i * halo_blocks - 1, 0), 0)),
            pl.BlockSpec((tm, D_MODEL), lambda i: (i, 0)),
            whole((D_MODEL, D_FF)), whole((D_MODEL, D_FF)),
            whole((3, D_FF)), whole((3, D_FF)), whole((1, D_FF)), whole((1, D_FF)),
            whole((D_FF, D_MODEL)), whole((1, D_MODEL)), whole((1, D_MODEL)),
        ],
        out_specs=pl.BlockSpec((tm, D_MODEL), lambda i: (i, 0)),
        out_shape=jax.ShapeDtypeStruct((n, D_MODEL), jnp.float32),
        scratch_shapes=[pltpu.VMEM((tm, D_FF), MXU_DTYPE)],
        compiler_params=pltpu.CompilerParams(
            dimension_semantics=("arbitrary",), vmem_limit_bytes=VMEM_LIMIT_BYTES),
        name="ffn",
    )(x2d, x2d, wa, wu, cwa, cwu, cba, cbu, wd, ln_g[None, :], ln_b[None, :])


_PQ0, _PQ1 = 0, A_Q_RANK
_PKV0, _PKV1 = _PQ1, _PQ1 + A_KV_RANK
_PKI0, _PKI1 = _PKV1, _PKV1 + LANES
_PWI0, _PWI1 = _PKI1, _PKI1 + LANES


def _dsa_proj_kernel(x_ref, wcat_ref, gq_ref, gkv_ref, gk_ref, bk_ref, wql_ref, wqi_ref,
                     qlat_ref, qidx_ref, ckv_ref, kidx_ref, widx_ref):
    xb = x_ref[...].astype(wcat_ref.dtype)
    proj = _dot(xb, wcat_ref[...])
    pq = proj[:, _PQ0:_PQ1]
    cq = pq * lax.rsqrt(jnp.mean(pq * pq, axis=-1, keepdims=True) + RMS_EPS) * gq_ref[...]
    pkv = proj[:, _PKV0:_PKV1]
    ckv = pkv * lax.rsqrt(jnp.mean(pkv * pkv, axis=-1, keepdims=True) + RMS_EPS) * gkv_ref[...]
    ckv_ref[...] = ckv.astype(ckv_ref.dtype)
    pk = proj[:, _PKI0:_PKI1]
    real = lax.broadcasted_iota(jnp.int32, pk.shape, 1) < IDX_DIM
    mu = jnp.sum(pk, axis=-1, keepdims=True) * (1.0 / IDX_DIM)
    c = jnp.where(real, pk - mu, 0.0)
    var = jnp.sum(c * c, axis=-1, keepdims=True) * (1.0 / IDX_DIM)
    kidx = c * lax.rsqrt(var + LN_EPS) * gk_ref[...] + bk_ref[...]
    kidx_ref[...] = kidx.astype(kidx_ref.dtype)
    widx_ref[...] = proj[:, _PWI0:_PWI1] * (IDX_HEADS ** -0.5 * IDX_DIM ** -0.5)
    cqb = cq.astype(wql_ref.dtype)
    ql = _dot(cqb, wql_ref[...]) * (A_KV_RANK ** -0.5 * LOG2_E)
    qi = _dot(cqb, wqi_ref[...])
    for h in range(A_HEADS):
        qlat_ref[0, h] = ql[:, h * A_KV_RANK:(h + 1) * A_KV_RANK].astype(qlat_ref.dtype)
    for h in range(IDX_HEADS):
        qidx_ref[0, h] = qi[:, h * IDX_PAD:(h + 1) * IDX_PAD].astype(qidx_ref.dtype)


def _dsa_proj(x2d, w_in, g_q, g_kv, w_q_lat, w_q_idx, g_kidx, b_kidx):
    n = x2d.shape[0]
    nblk = n // TQ
    o1, o2, o3 = A_Q_RANK, A_Q_RANK + A_KV_RANK, A_Q_RANK + A_KV_RANK + IDX_DIM
    zpad = lambda w, cols: jnp.pad(w, ((0, 0), (0, cols - w.shape[1])))
    wcat = jnp.concatenate(
        [w_in[:, :o2], zpad(w_in[:, o2:o3], LANES), zpad(w_in[:, o3:], LANES)], axis=1).astype(MXU_DTYPE)
    pcols = wcat.shape[1]
    gk = zpad(g_kidx[None, :], LANES)
    bk = zpad(b_kidx[None, :], LANES)
    wqi = jnp.pad(w_q_idx.reshape(A_Q_RANK, IDX_HEADS, IDX_DIM), ((0, 0), (0, 0), (0, IDX_PAD - IDX_DIM)))
    wqi = wqi.reshape(A_Q_RANK, IDX_HEADS * IDX_PAD).astype(MXU_DTYPE)
    wql = w_q_lat.astype(MXU_DTYPE)
    full = lambda shape: pl.BlockSpec(shape, lambda i: (0,) * len(shape))
    return pl.pallas_call(
        _dsa_proj_kernel,
        grid=(nblk,),
        in_specs=[
            pl.BlockSpec((TQ, D_MODEL), lambda i: (i, 0)),
            full((D_MODEL, pcols)), full((1, A_Q_RANK)), full((1, A_KV_RANK)), full((1, LANES)), full((1, LANES)),
            full((A_Q_RANK, A_HEADS * A_KV_RANK)), full((A_Q_RANK, IDX_HEADS * IDX_PAD)),
        ],
        out_specs=[
            pl.BlockSpec((1, A_HEADS, TQ, A_KV_RANK), lambda i: (i, 0, 0, 0)),
            pl.BlockSpec((1, IDX_HEADS, TQ, IDX_PAD), lambda i: (i, 0, 0, 0)),
            pl.BlockSpec((TQ, A_KV_RANK), lambda i: (i, 0)),
            pl.BlockSpec((TQ, IDX_PAD), lambda i: (i, 0)),
            pl.BlockSpec((TQ, LANES), lambda i: (i, 0)),
        ],
        out_shape=[
            jax.ShapeDtypeStruct((nblk, A_HEADS, TQ, A_KV_RANK), MXU_DTYPE),
            jax.ShapeDtypeStruct((nblk, IDX_HEADS, TQ, IDX_PAD), MXU_DTYPE),
            jax.ShapeDtypeStruct((n, A_KV_RANK), MXU_DTYPE),
            jax.ShapeDtypeStruct((n, IDX_PAD), MXU_DTYPE),
            jax.ShapeDtypeStruct((n, LANES), jnp.float32),
        ],
        compiler_params=pltpu.CompilerParams(
            dimension_semantics=("arbitrary",), vmem_limit_bytes=VMEM_LIMIT_BYTES),
        name="dsa_proj",
    )(x2d, wcat, g_q[None, :], g_kv[None, :], gk, bk, wql, wqi)


def _key_to_f32(key):
    bits = key ^ ((key >> 31) & jnp.int32(0x7FFFFFFF))
    return lax.bitcast_convert_type(bits, jnp.float32)


def _dsa_attn_kernel(qlat_ref, qidx_ref, widx_ref, ckv_ref, kidx_ref, x_ref, wuv_ref, wout_ref, g_ref, b_ref,
                     o_ref, sc_ref, sct_ref, m_ref, l_ref, acc_ref, *, top_k):
    i = pl.program_id(1)
    n_tiles = i + 1
    rows = A_HEADS * TQ
    kf = jnp.float32(top_k)
    row_id = lax.broadcasted_iota(jnp.int32, (TQ, TQ), 0)
    col_id = lax.broadcasted_iota(jnp.int32, (TQ, TQ), 1)

    qi = qidx_ref[0].reshape(IDX_HEADS * TQ, IDX_PAD)
    w = widx_ref[...]

    def score_tiles(jj, carry):
        for u in range(ATTN_SPAN):
            j = ATTN_SPAN * jj + u
            kt = kidx_ref[0, pl.ds(pl.multiple_of(j * TQ, TQ), TQ), :]
            z = _dot_nt(qi, kt)
            s = jnp.zeros((TQ, TQ), jnp.float32)
            for h in range(IDX_HEADS):
                s = s + w[:, h:h + 1] * jnp.maximum(z[h * TQ:(h + 1) * TQ, :], 0.0)
            valid = (j * TQ + col_id) <= (i * TQ + row_id)
            s = jnp.where(valid, s, SCORE_MASKED)
            sc_ref[j] = s
            sct_ref[j] = s.T
        return carry

    lax.fori_loop(0, (i + ATTN_SPAN) // ATTN_SPAN, score_tiles, 0)

    def count(pred):
        def body(j, acc):
            hit = jnp.where(pred(sct_ref[j]), 1.0, 0.0)
            return acc + jnp.sum(hit.reshape(TQ // COUNT_ROWS, COUNT_ROWS, TQ), axis=0)
        acc = lax.fori_loop(0, n_tiles, body, jnp.zeros((COUNT_ROWS, TQ), jnp.float32))
        return jnp.sum(acc, axis=0, keepdims=True)

    n_nonneg = count(lambda s: s >= 0.0)
    nonneg = n_nonneg >= kf
    key0 = jnp.where(nonneg, jnp.int32(0), jnp.int32(-2 ** 31))
    n_ge0 = jnp.where(nonneg, n_nonneg, (n_tiles * TQ).astype(jnp.float32))
    short = (i * TQ + 1 + lax.broadcasted_iota(jnp.int32, (1, TQ), 1)) < top_k

    def bit_step(it, state):
        key, n_key = state
        cand = key | lax.shift_left(jnp.int32(1), 30 - it)
        cand_f = _key_to_f32(cand)
        n_ge = count(lambda s: s >= cand_f)
        accept = n_ge >= kf
        return jnp.where(accept, cand, key), jnp.where(accept, n_ge, n_key)

    def unsettled(n_key):
        return jnp.max(jnp.where((n_key == kf) | short, 0, 1))

    def late_step(state):
        it, _, key, n_key = state
        key, n_key = bit_step(it, (key, n_key))
        return it + 1, unsettled(n_key), key, n_key

    key, n_key = lax.fori_loop(0, SEARCH_FIXED_STEPS, bit_step, (key0, n_ge0))
    _, _, key, _ = lax.while_loop(lambda st: (st[1] > 0) & (st[0] < SEARCH_STEPS), late_step,
                                  (jnp.int32(SEARCH_FIXED_STEPS), unsettled(n_key), key, n_key))
    thr = jnp.where(short, SCORE_MASKED, _key_to_f32(key))
    need = jnp.where(short, 0.0, kf - count(lambda s: s > thr))

    def to_cols(v):
        return jnp.broadcast_to(v, (LANES, TQ)).T

    thr_c = to_cols(thr)
    need_c = to_cols(need)

    tri = (row_id <= col_id).astype(wuv_ref.dtype)
    thr2 = jnp.concatenate([thr_c, thr_c], axis=1)
    need2 = jnp.concatenate([need_c, need_c], axis=1)

    def bias_tile(j, seen):
        s = sc_ref[j]
        eq = s == thr2
        prefix = _dot(jnp.where(eq, 1.0, 0.0).astype(tri.dtype), tri) + seen
        take = (s > thr2) | (eq & (prefix <= need2))
        sc_ref[j] = jnp.where(take, 0.0, BIAS_MASKED)
        return prefix[:, TQ - 1:TQ]

    lax.fori_loop(0, n_tiles, bias_tile, jnp.zeros((TQ, 1), jnp.float32))

    @pl.when(i % ATTN_SPAN == 0)
    def _():
        sc_ref[i + 1] = jnp.full((TQ, TQ), BIAS_MASKED, jnp.float32)

    q = qlat_ref[0].reshape(rows, A_KV_RANK)
    m_ref[...] = jnp.full(m_ref.shape, BIAS_MASKED, jnp.float32)
    l_ref[...] = jnp.zeros_like(l_ref)
    acc_ref[...] = jnp.zeros_like(acc_ref)
    tk = ATTN_SPAN * TQ

    def lanes(v, width):
        return jnp.concatenate([v] * (width // LANES), axis=1)

    def attn_tile(jj, carry):
        kv = ckv_ref[0, pl.ds(pl.multiple_of(jj * tk, tk), tk), :]
        bias = jnp.concatenate([sc_ref[ATTN_SPAN * jj + u] for u in range(ATTN_SPAN)], axis=1)
        for h in range(A_HEADS):
            hs = slice(h * TQ, (h + 1) * TQ)
            logits = _dot_nt(q[hs], kv) + bias
            m_old = m_ref[hs, :]
            m_new = jnp.maximum(m_old, jnp.max(logits, axis=1, keepdims=True))
            p = jnp.exp2(logits - lanes(m_new, tk))
            alpha = jnp.exp2(m_old - m_new)
            p_lanes = sum(p[:, u * LANES:(u + 1) * LANES] for u in range(tk // LANES))
            l_ref[hs, :] = alpha * l_ref[hs, :] + p_lanes
            m_ref[hs, :] = m_new
            acc_ref[hs, :] = lanes(alpha, A_KV_RANK) * acc_ref[hs, :] + _dot(p.astype(kv.dtype), kv)
        return carry

    lax.fori_loop(0, (i + ATTN_SPAN) // ATTN_SPAN, attn_tile, 0)

    o_lat = acc_ref[...] / jnp.sum(l_ref[...], axis=1, keepdims=True)
    heads = []
    for h in range(A_HEADS):
        heads.append(_dot(o_lat[h * TQ:(h + 1) * TQ, :].astype(wuv_ref.dtype), wuv_ref[h]))
    o = jnp.concatenate(heads, axis=1)
    mix = _dot(o.astype(wout_ref.dtype), wout_ref[...])
    o_ref[0] = _layer_norm(DN_ALPHA * x_ref[0] + mix, g_ref[...], b_ref[...])


def _dsa_attn(x, qlat, qidx, widx, ckv, kidx, w_uv, w_out, ln_g, ln_b):
    bsz, seq, _ = x.shape
    nq = seq // TQ
    top_k = min(TOPK_MAX, seq // 4)
    kern = functools.partial(_dsa_attn_kernel, top_k=top_k)
    rows = A_HEADS * TQ
    return pl.pallas_call(
        kern,
        grid=(bsz, nq),
        in_specs=[
            pl.BlockSpec((1, A_HEADS, TQ, A_KV_RANK), lambda b, i: (b * nq + i, 0, 0, 0)),
            pl.BlockSpec((1, IDX_HEADS, TQ, IDX_PAD), lambda b, i: (b * nq + i, 0, 0, 0)),
            pl.BlockSpec((TQ, LANES), lambda b, i: (b * nq + i, 0)),
            pl.BlockSpec((1, seq, A_KV_RANK), lambda b, i: (b, 0, 0)),
            pl.BlockSpec((1, seq, IDX_PAD), lambda b, i: (b, 0, 0)),
            pl.BlockSpec((1, TQ, D_MODEL), lambda b, i: (b, i, 0)),
            pl.BlockSpec((A_HEADS, A_KV_RANK, A_HEAD_DIM), lambda b, i: (0, 0, 0)),
            pl.BlockSpec((D_MODEL, D_MODEL), lambda b, i: (0, 0)),
            pl.BlockSpec((1, D_MODEL), lambda b, i: (0, 0)),
            pl.BlockSpec((1, D_MODEL), lambda b, i: (0, 0)),
        ],
        out_specs=pl.BlockSpec((1, TQ, D_MODEL), lambda b, i: (b, i, 0)),
        out_shape=jax.ShapeDtypeStruct((bsz, seq, D_MODEL), jnp.float32),
        scratch_shapes=[
            pltpu.VMEM((nq, TQ, TQ), jnp.float32),
            pltpu.VMEM((nq, TQ, TQ), jnp.float32),
            pltpu.VMEM((rows, LANES), jnp.float32),
            pltpu.VMEM((rows, LANES), jnp.float32),
            pltpu.VMEM((rows, A_KV_RANK), jnp.float32),
        ],
        compiler_params=pltpu.CompilerParams(
            dimension_semantics=("arbitrary", "arbitrary"), vmem_limit_bytes=VMEM_LIMIT_BYTES),
        name="dsa_attn",
    )(qlat, qidx, widx, ckv.reshape(bsz, seq, A_KV_RANK), kidx.reshape(bsz, seq, IDX_PAD), x,
      w_uv.astype(MXU_DTYPE), w_out.astype(MXU_DTYPE), ln_g[None, :], ln_b[None, :])


def _dsa_layer(x, w_in, g_q, g_kv, w_q_lat, w_q_idx, g_kidx, b_kidx, w_uv, w_out, ln_g, ln_b):
    bsz, seq, _ = x.shape
    qlat, qidx, ckv, kidx, widx = _dsa_proj(x.reshape(bsz * seq, D_MODEL), w_in, g_q, g_kv, w_q_lat, w_q_idx,
                                            g_kidx, b_kidx)
    return _dsa_attn(x, qlat, qidx, widx, ckv, kidx, w_uv, w_out, ln_g, ln_b)


def _hgrn_kernel(x_ref, win_ref, lb_ref, go_ref, wout_ref, g_ref, b_ref, o_ref,
                 st_ref, qp_ref, kp_ref, on_ref):
    t_len = HG_T

    @pl.when(pl.program_id(1) == 0)
    def _():
        st_ref[...] = jnp.zeros_like(st_ref)

    xb = x_ref[0].astype(win_ref.dtype)
    row = lax.broadcasted_iota(jnp.int32, (t_len, B_DIM), 0)
    in_grp = row % SUBLANES
    rr = lax.broadcasted_iota(jnp.int32, (t_len, t_len), 0)
    cc = lax.broadcasted_iota(jnp.int32, (t_len, t_len), 1)
    rc_xor = rr ^ cc

    def head(h, carry):
        p = _dot(xb, win_ref[h])
        q_raw, f_raw, v, g_raw = (p[:, k * B_DIM:(k + 1) * B_DIM] for k in range(4))
        lb = lb_ref[h]
        forget = lb + (1.0 - lb) * _sigmoid(f_raw)
        log_f = jnp.log(forget)
        k = (1.0 - lb) * _sigmoid(-f_raw)
        q = q_raw * _sigmoid(q_raw)

        b = log_f
        sh = 1
        while sh < t_len:
            b = b + jnp.where(row >= sh, pltpu.roll(b, sh, axis=0), 0.0)
            sh *= 2
        b_last = b[t_len - 1:t_len, :]

        st = st_ref[h]
        o = _dot_nt((q * jnp.exp(b)).astype(xb.dtype), st.astype(xb.dtype))
        k_tail = (k * jnp.exp(b_last - b)).astype(xb.dtype)
        st_ref[h] = st * jnp.exp(b_last) + _dot_tn(v.astype(xb.dtype), k_tail)

        def grp_roll(a, j):
            return pltpu.roll(a.reshape(t_len // SUBLANES, SUBLANES, B_DIM), j, axis=1).reshape(t_len, B_DIM)

        prod = jnp.ones_like(forget)
        for j in range(SUBLANES):
            if j == 0:
                kj, vj = k, v
            else:
                kj, vj = grp_roll(k, j), grp_roll(v, j)
                prod = prod * (forget if j == 1 else grp_roll(forget, j - 1))
            term = jnp.where(in_grp >= j, q * kj * prod, 0.0)
            o = o + jnp.sum(term, axis=-1, keepdims=True) * vj

        for li, c in enumerate(HG_LEVELS):
            for m in range(c, t_len, 2 * c):
                ref_b = b[m - 1:m, :]
                qp_ref[li, m:m + c, :] = q[m:m + c, :] * jnp.exp(b[m:m + c, :] - ref_b)
                qp_ref[li, m - c:m, :] = jnp.zeros((c, B_DIM), jnp.float32)
                kp_ref[li, m - c:m, :] = k[m - c:m, :] * jnp.exp(ref_b - b[m - c:m, :])
                kp_ref[li, m:m + c, :] = jnp.zeros((c, B_DIM), jnp.float32)
        a = None
        for li in reversed(range(len(HG_LEVELS))):
            al = _dot_nt(qp_ref[li].astype(xb.dtype), kp_ref[li].astype(xb.dtype))
            a = al if a is None else jnp.where(rc_xor < 2 * HG_LEVELS[li], al, a)
        o = o + _dot(a.astype(xb.dtype), v.astype(xb.dtype))

        og = o * _sigmoid(g_raw)
        on = og * lax.rsqrt(jnp.mean(og * og, axis=-1, keepdims=True) + RMS_EPS) * go_ref[h]
        on_ref[h] = on.astype(on_ref.dtype)
        return carry

    lax.fori_loop(0, B_HEADS, head, 0, unroll=4)

    on = jnp.concatenate([on_ref[h] for h in range(B_HEADS)], axis=1)
    mix = _dot(on, wout_ref[...])
    o_ref[0] = _layer_norm(DN_ALPHA * x_ref[0] + mix, g_ref[...], b_ref[...])


def _hgrn_layer(x, w_in, lb, g_o, w_out, ln_g, ln_b):
    bsz, seq, _ = x.shape
    win = w_in.reshape(D_MODEL, 4, B_HEADS, B_DIM).transpose(2, 0, 1, 3).reshape(B_HEADS, D_MODEL, 4 * B_DIM)
    n_lv = len(HG_LEVELS)
    return pl.pallas_call(
        _hgrn_kernel,
        grid=(bsz, seq // HG_T),
        in_specs=[
            pl.BlockSpec((1, HG_T, D_MODEL), lambda b, i: (b, i, 0)),
            pl.BlockSpec((B_HEADS, D_MODEL, 4 * B_DIM), lambda b, i: (0, 0, 0)),
            pl.BlockSpec((B_HEADS, 1, B_DIM), lambda b, i: (0, 0, 0)),
            pl.BlockSpec((B_HEADS, 1, B_DIM), lambda b, i: (0, 0, 0)),
            pl.BlockSpec((D_MODEL, D_MODEL), lambda b, i: (0, 0)),
            pl.BlockSpec((1, D_MODEL), lambda b, i: (0, 0)),
            pl.BlockSpec((1, D_MODEL), lambda b, i: (0, 0)),
        ],
        out_specs=pl.BlockSpec((1, HG_T, D_MODEL), lambda b, i: (b, i, 0)),
        out_shape=jax.ShapeDtypeStruct((bsz, seq, D_MODEL), jnp.float32),
        scratch_shapes=[
            pltpu.VMEM((B_HEADS, B_DIM, B_DIM), jnp.float32),
            pltpu.VMEM((n_lv, HG_T, B_DIM), jnp.float32),
            pltpu.VMEM((n_lv, HG_T, B_DIM), jnp.float32),
            pltpu.VMEM((B_HEADS, HG_T, B_DIM), MXU_DTYPE),
        ],
        compiler_params=pltpu.CompilerParams(
            dimension_semantics=("arbitrary", "arbitrary"), vmem_limit_bytes=VMEM_LIMIT_BYTES),
        name="hgrn",
    )(x, win.astype(MXU_DTYPE), lb.reshape(B_HEADS, 1, B_DIM), g_o.reshape(B_HEADS, 1, B_DIM),
      w_out.astype(MXU_DTYPE), ln_g[None, :], ln_b[None, :])


@jax.jit
def kernel(x, a_w_in, a_g_q, a_g_kv, a_w_q_lat, a_w_q_idx, a_g_kidx, a_b_kidx, a_w_uv, a_w_out, b_w_in, b_lb_logits, b_g_o, b_w_out, ln1_g, ln1_b, f_w_up, f_conv_w, f_conv_b, f_w_down, ln2_g, ln2_b):
    bsz, seq, _ = x.shape
    c = jnp.cumsum(jax.nn.softmax(b_lb_logits.astype(jnp.float32), axis=0), axis=0)
    lower_bounds = c - c[0:1]
    for layer in range(DEPTH):
        j = layer // N_MIXERS
        if layer % N_MIXERS == 0:
            x = _dsa_layer(x, a_w_in[j], a_g_q[j], a_g_kv[j], a_w_q_lat[j], a_w_q_idx[j], a_g_kidx[j],
                           a_b_kidx[j], a_w_uv[j], a_w_out[j], ln1_g[layer], ln1_b[layer])
        else:
            x = _hgrn_layer(x, b_w_in[j], lower_bounds[layer], b_g_o[j], b_w_out[j], ln1_g[layer], ln1_b[layer])
        x = _ffn_layer(x.reshape(bsz * seq, D_MODEL), seq, f_w_up[layer], f_conv_w[layer], f_conv_b[layer],
                       f_w_down[layer], ln2_g[layer], ln2_b[layer]).reshape(bsz, seq, D_MODEL)
    return x
```

```python
import functools

import jax
import jax.numpy as jnp
from jax import lax
from jax.experimental import pallas as pl
from jax.experimental.pallas import tpu as pltpu

D_MODEL = 1024
DEPTH = 4
N_MIXERS = 2
A_HEADS = 8
A_HEAD_DIM = D_MODEL // A_HEADS
A_Q_RANK = 384
A_KV_RANK = 256
IDX_HEADS = 8
IDX_DIM = 64
TOPK_MAX = 256
B_HEADS = 8
B_DIM = D_MODEL // B_HEADS
D_FF = 2816
DN_ALPHA = (2 * DEPTH) ** 0.25
LN_EPS = 1e-5
RMS_EPS = 1e-6
LOG2_E = 1.4426950408889634

LANES = 128
SUBLANES = 8
VMEM_LIMIT_BYTES = 56 * 1024 * 1024

MXU_DTYPE = jnp.bfloat16

IDX_PAD = LANES
TQ = 256
ATTN_SPAN = 2
HG_T = 256
HG_LEVELS = (8, 16, 32, 64, 128)
FFN_TM = 512
FFN_HALO = SUBLANES
FFN_SLAB = 256

SEARCH_STEPS = 31
SEARCH_FIXED_STEPS = 24
COUNT_ROWS = 4 * SUBLANES
SCORE_MASKED = -3.0e38
SCORE_ABOVE_MASKED = -2.9e38
BIAS_MASKED = -1.0e30


def _dot(a, b):
    return jnp.dot(a, b, preferred_element_type=jnp.float32)


def _dot_nt(a, b):
    return lax.dot_general(a, b, (((1,), (1,)), ((), ())), preferred_element_type=jnp.float32)


def _dot_tn(a, b):
    return lax.dot_general(a, b, (((0,), (0,)), ((), ())), preferred_element_type=jnp.float32)


def _layer_norm(v, g, b):
    mu = jnp.mean(v, axis=-1, keepdims=True)
    c = v - mu
    var = jnp.mean(c * c, axis=-1, keepdims=True)
    return c * lax.rsqrt(var + LN_EPS) * g + b


def _sigmoid(v):
    return 1.0 / (1.0 + jnp.exp(-v))


def _ffn_kernel(xh_ref, x_ref, wa_ref, wu_ref, cwa_ref, cwu_ref, cba_ref, cbu_ref, wd_ref, g_ref, b_ref,
                o_ref, hid_ref, *, tiles_per_seq):
    i = pl.program_id(0)
    first = (i % tiles_per_seq) == 0
    halo = jnp.where(first, 0.0, xh_ref[...])
    xb = jnp.concatenate([halo, x_ref[...]], axis=0).astype(wa_ref.dtype)

    def conv(h, cw, cb):
        h1 = pltpu.roll(h, 1, axis=0)
        h2 = pltpu.roll(h, 2, axis=0)
        y = h * cw[2:3, :] + h1 * cw[1:2, :] + h2 * cw[0:1, :] + cb
        return y[FFN_HALO:, :]

    for c in range(D_FF // FFN_SLAB):
        cs = slice(c * FFN_SLAB, (c + 1) * FFN_SLAB)
        a = conv(_dot(xb, wa_ref[:, cs]), cwa_ref[:, cs], cba_ref[:, cs])
        u = conv(_dot(xb, wu_ref[:, cs]), cwu_ref[:, cs], cbu_ref[:, cs])
        hid_ref[:, cs] = ((a * _sigmoid(a)) * u).astype(hid_ref.dtype)
    mix = _dot(hid_ref[...], wd_ref[...])
    o_ref[...] = _layer_norm(DN_ALPHA * x_ref[...] + mix, g_ref[...], b_ref[...])


def _ffn_layer(x2d, seq, w_up, conv_w, conv_b, w_down, ln_g, ln_b):
    n = x2d.shape[0]
    tm = FFN_TM
    wa = w_up[:, :D_FF].astype(MXU_DTYPE)
    wu = w_up[:, D_FF:].astype(MXU_DTYPE)
    cw = conv_w[:, 0, :]
    cwa, cwu = cw[:, :D_FF], cw[:, D_FF:]
    cba, cbu = conv_b[None, :D_FF], conv_b[None, D_FF:]
    wd = w_down.astype(MXU_DTYPE)
    halo_blocks = tm // FFN_HALO
    kern = functools.partial(_ffn_kernel, tiles_per_seq=seq // tm)
    whole = lambda shape: pl.BlockSpec(shape, lambda i: (0,) * len(shape), pipeline_mode=pl.Buffered(1))
    return pl.pallas_call(
        kern,
        grid=(n // tm,),
        in_specs=[
            pl.BlockSpec((FFN_HALO, D_MODEL), lambda i: (jnp.maximum(i * halo_blocks - 1, 0), 0)),
            pl.BlockSpec((tm, D_MODEL), lambda i: (i, 0)),
            whole((D_MODEL, D_FF)), whole((D_MODEL, D_FF)),
            whole((3, D_FF)), whole((3, D_FF)), whole((1, D_FF)), whole((1, D_FF)),
            whole((D_FF, D_MODEL)), whole((1, D_MODEL)), whole((1, D_MODEL)),
        ],
        out_specs=pl.BlockSpec((tm, D_MODEL), lambda i: (i, 0)),
        out_shape=jax.ShapeDtypeStruct((n, D_MODEL), jnp.float32),
        scratch_shapes=[pltpu.VMEM((tm, D_FF), MXU_DTYPE)],
        compiler_params=pltpu.CompilerParams(
            dimension_semantics=("arbitrary",), vmem_limit_bytes=VMEM_LIMIT_BYTES),
        name="ffn",
    )(x2d, x2d, wa, wu, cwa, cwu, cba, cbu, wd, ln_g[None, :], ln_b[None, :])


_PQ0, _PQ1 = 0, A_Q_RANK
_PKV0, _PKV1 = _PQ1, _PQ1 + A_KV_RANK
_PKI0, _PKI1 = _PKV1, _PKV1 + LANES
_PWI0, _PWI1 = _PKI1, _PKI1 + LANES


def _dsa_proj_kernel(x_ref, wcat_ref, gq_ref, gkv_ref, gk_ref, bk_ref, wql_ref, wqi_ref,
                     qlat_ref, qidx_ref, ckv_ref, kidx_ref, widx_ref):
    xb = x_ref[...].astype(wcat_ref.dtype)
    proj = _dot(xb, wcat_ref[...])
    pq = proj[:, _PQ0:_PQ1]
    cq = pq * lax.rsqrt(jnp.mean(pq * pq, axis=-1, keepdims=True) + RMS_EPS) * gq_ref[...]
    pkv = proj[:, _PKV0:_PKV1]
    ckv = pkv * lax.rsqrt(jnp.mean(pkv * pkv, axis=-1, keepdims=True) + RMS_EPS) * gkv_ref[...]
    ckv_ref[...] = ckv.astype(ckv_ref.dtype)
    pk = proj[:, _PKI0:_PKI1]
    real = lax.broadcasted_iota(jnp.int32, pk.shape, 1) < IDX_DIM
    mu = jnp.sum(pk, axis=-1, keepdims=True) * (1.0 / IDX_DIM)
    c = jnp.where(real, pk - mu, 0.0)
    var = jnp.sum(c * c, axis=-1, keepdims=True) * (1.0 / IDX_DIM)
    kidx = c * lax.rsqrt(var + LN_EPS) * gk_ref[...] + bk_ref[...]
    kidx_ref[...] = kidx.astype(kidx_ref.dtype)
    widx_ref[...] = proj[:, _PWI0:_PWI1] * (IDX_HEADS ** -0.5 * IDX_DIM ** -0.5)
    cqb = cq.astype(wql_ref.dtype)
    ql = _dot(cqb, wql_ref[...]) * (A_KV_RANK ** -0.5 * LOG2_E)
    qi = _dot(cqb, wqi_ref[...])
    for h in range(A_HEADS):
        qlat_ref[0, h] = ql[:, h * A_KV_RANK:(h + 1) * A_KV_RANK].astype(qlat_ref.dtype)
    for h in range(IDX_HEADS):
        qidx_ref[0, h] = qi[:, h * IDX_PAD:(h + 1) * IDX_PAD].astype(qidx_ref.dtype)


def _dsa_proj(x2d, w_in, g_q, g_kv, w_q_lat, w_q_idx, g_kidx, b_kidx):
    n = x2d.shape[0]
    nblk = n // TQ
    o1, o2, o3 = A_Q_RANK, A_Q_RANK + A_KV_RANK, A_Q_RANK + A_KV_RANK + IDX_DIM
    zpad = lambda w, cols: jnp.pad(w, ((0, 0), (0, cols - w.shape[1])))
    wcat = jnp.concatenate(
        [w_in[:, :o2], zpad(w_in[:, o2:o3], LANES), zpad(w_in[:, o3:], LANES)], axis=1).astype(MXU_DTYPE)
    pcols = wcat.shape[1]
    gk = zpad(g_kidx[None, :], LANES)
    bk = zpad(b_kidx[None, :], LANES)
    wqi = jnp.pad(w_q_idx.reshape(A_Q_RANK, IDX_HEADS, IDX_DIM), ((0, 0), (0, 0), (0, IDX_PAD - IDX_DIM)))
    wqi = wqi.reshape(A_Q_RANK, IDX_HEADS * IDX_PAD).astype(MXU_DTYPE)
    wql = w_q_lat.astype(MXU_DTYPE)
    full = lambda shape: pl.BlockSpec(shape, lambda i: (0,) * len(shape))
    return pl.pallas_call(
        _dsa_proj_kernel,
        grid=(nblk,),
        in_specs=[
            pl.BlockSpec((TQ, D_MODEL), lambda i: (i, 0)),
            full((D_MODEL, pcols)), full((1, A_Q_RANK)), full((1, A_KV_RANK)), full((1, LANES)), full((1, LANES)),
            full((A_Q_RANK, A_HEADS * A_KV_RANK)), full((A_Q_RANK, IDX_HEADS * IDX_PAD)),
        ],
        out_specs=[
            pl.BlockSpec((1, A_HEADS, TQ, A_KV_RANK), lambda i: (i, 0, 0, 0)),
            pl.BlockSpec((1, IDX_HEADS, TQ, IDX_PAD), lambda i: (i, 0, 0, 0)),
            pl.BlockSpec((TQ, A_KV_RANK), lambda i: (i, 0)),
            pl.BlockSpec((TQ, IDX_PAD), lambda i: (i, 0)),
            pl.BlockSpec((TQ, LANES), lambda i: (i, 0)),
        ],
        out_shape=[
            jax.ShapeDtypeStruct((nblk, A_HEADS, TQ, A_KV_RANK), MXU_DTYPE),
            jax.ShapeDtypeStruct((nblk, IDX_HEADS, TQ, IDX_PAD), MXU_DTYPE),
            jax.ShapeDtypeStruct((n, A_KV_RANK), MXU_DTYPE),
            jax.ShapeDtypeStruct((n, IDX_PAD), MXU_DTYPE),
            jax.ShapeDtypeStruct((n, LANES), jnp.float32),
        ],
        compiler_params=pltpu.CompilerParams(
            dimension_semantics=("arbitrary",), vmem_limit_bytes=VMEM_LIMIT_BYTES),
        name="dsa_proj",
    )(x2d, wcat, g_q[None, :], g_kv[None, :], gk, bk, wql, wqi)


def _key_to_f32(key):
    bits = key ^ ((key >> 31) & jnp.int32(0x7FFFFFFF))
    return lax.bitcast_convert_type(bits, jnp.float32)


def _dsa_attn_kernel(qlat_ref, qidx_ref, widx_ref, ckv_ref, kidx_ref, x_ref, wuv_ref, wout_ref, g_ref, b_ref,
                     o_ref, sc_ref, sct_ref, m_ref, l_ref, acc_ref, *, top_k):
    i = pl.program_id(1)
    n_tiles = i + 1
    rows = A_HEADS * TQ
    kf = jnp.float32(top_k)
    row_id = lax.broadcasted_iota(jnp.int32, (TQ, TQ), 0)
    col_id = lax.broadcasted_iota(jnp.int32, (TQ, TQ), 1)

    qi = qidx_ref[0].reshape(IDX_HEADS * TQ, IDX_PAD)
    w = widx_ref[...]

    def score_tiles(jj, carry):
        for u in range(ATTN_SPAN):
            j = ATTN_SPAN * jj + u
            kt = kidx_ref[0, pl.ds(pl.multiple_of(j * TQ, TQ), TQ), :]
            z = _dot_nt(qi, kt)
            s = jnp.zeros((TQ, TQ), jnp.float32)
            for h in range(IDX_HEADS):
                s = s + w[:, h:h + 1] * jnp.maximum(z[h * TQ:(h + 1) * TQ, :], 0.0)
            valid = (j * TQ + col_id) <= (i * TQ + row_id)
            s = jnp.where(valid, s, SCORE_MASKED)
            sc_ref[j] = s
            sct_ref[j] = s.T
        return carry

    lax.fori_loop(0, (i + ATTN_SPAN) // ATTN_SPAN, score_tiles, 0)

    def count(pred):
        def body(j, acc):
            hit = jnp.where(pred(sct_ref[j]), 1.0, 0.0)
            return acc + jnp.sum(hit.reshape(TQ // COUNT_ROWS, COUNT_ROWS, TQ), axis=0)
        acc = lax.fori_loop(0, n_tiles, body, jnp.zeros((COUNT_ROWS, TQ), jnp.float32))
        return jnp.sum(acc, axis=0, keepdims=True)

    n_nonneg = count(lambda s: s >= 0.0)
    nonneg = n_nonneg >= kf
    key0 = jnp.where(nonneg, jnp.int32(0), jnp.int32(-2 ** 31))
    n_ge0 = jnp.where(nonneg, n_nonneg, (n_tiles * TQ).astype(jnp.float32))
    short = (i * TQ + 1 + lax.broadcasted_iota(jnp.int32, (1, TQ), 1)) < top_k

    def bit_step(it, state):
        key, n_key = state
        cand = key | lax.shift_left(jnp.int32(1), 30 - it)
        cand_f = _key_to_f32(cand)
        n_ge = count(lambda s: s >= cand_f)
        accept = n_ge >= kf
        return jnp.where(accept, cand, key), jnp.where(accept, n_ge, n_key)

    def unsettled(n_key):
        return jnp.max(jnp.where((n_key == kf) | short, 0, 1))

    def late_step(state):
        it, _, key, n_key = state
        key, n_key = bit_step(it, (key, n_key))
        return it + 1, unsettled(n_key), key, n_key

    key, n_key = lax.fori_loop(0, SEARCH_FIXED_STEPS, bit_step, (key0, n_ge0))
    _, still_unsettled, key, _ = lax.while_loop(
        lambda st: (st[1] > 0) & (st[0] < SEARCH_STEPS), late_step,
        (jnp.int32(SEARCH_FIXED_STEPS), unsettled(n_key), key, n_key))
    thr = jnp.where(short, SCORE_MASKED, _key_to_f32(key))

    def to_tile(v):
        cols = jnp.broadcast_to(v, (LANES, TQ)).T
        return jnp.concatenate([cols] * (TQ // LANES), axis=1)

    def bias_settled():
        floor = to_tile(jnp.where(short, SCORE_ABOVE_MASKED, thr))

        def tile(j, carry):
            sc_ref[j] = jnp.where(sc_ref[j] >= floor, 0.0, BIAS_MASKED)
            return carry

        lax.fori_loop(0, n_tiles, tile, 0)
        return 0

    def bias_with_ties():
        need2 = to_tile(jnp.where(short, 0.0, kf - count(lambda s: s > thr)))
        thr2 = to_tile(thr)
        tri = (row_id <= col_id).astype(wuv_ref.dtype)

        def tile(j, seen):
            s = sc_ref[j]
            eq = s == thr2
            prefix = _dot(jnp.where(eq, 1.0, 0.0).astype(tri.dtype), tri) + seen
            take = (s > thr2) | (eq & (prefix <= need2))
            sc_ref[j] = jnp.where(take, 0.0, BIAS_MASKED)
            return prefix[:, TQ - 1:TQ]

        lax.fori_loop(0, n_tiles, tile, jnp.zeros((TQ, 1), jnp.float32))
        return 0

    lax.cond(still_unsettled == 0, bias_settled, bias_with_ties)

    @pl.when(i % ATTN_SPAN == 0)
    def _():
        sc_ref[i + 1] = jnp.full((TQ, TQ), BIAS_MASKED, jnp.float32)

    q = qlat_ref[0].reshape(rows, A_KV_RANK)
    m_ref[...] = jnp.full(m_ref.shape, BIAS_MASKED, jnp.float32)
    l_ref[...] = jnp.zeros_like(l_ref)
    acc_ref[...] = jnp.zeros_like(acc_ref)
    tk = ATTN_SPAN * TQ

    def lanes(v, width):
        return jnp.concatenate([v] * (width // LANES), axis=1)

    def attn_tile(jj, carry):
        kv = ckv_ref[0, pl.ds(pl.multiple_of(jj * tk, tk), tk), :]
        bias = jnp.concatenate([sc_ref[ATTN_SPAN * jj + u] for u in range(ATTN_SPAN)], axis=1)
        for h in range(A_HEADS):
            hs = slice(h * TQ, (h + 1) * TQ)
            logits = _dot_nt(q[hs], kv) + bias
            m_old = m_ref[hs, :]
            m_new = jnp.maximum(m_old, jnp.max(logits, axis=1, keepdims=True))
            p = jnp.exp2(logits - lanes(m_new, tk))
            alpha = jnp.exp2(m_old - m_new)
            p_lanes = sum(p[:, u * LANES:(u + 1) * LANES] for u in range(tk // LANES))
            l_ref[hs, :] = alpha * l_ref[hs, :] + p_lanes
            m_ref[hs, :] = m_new
            acc_ref[hs, :] = lanes(alpha, A_KV_RANK) * acc_ref[hs, :] + _dot(p.astype(kv.dtype), kv)
        return carry

    lax.fori_loop(0, (i + ATTN_SPAN) // ATTN_SPAN, attn_tile, 0)

    o_lat = acc_ref[...] / jnp.sum(l_ref[...], axis=1, keepdims=True)
    heads = []
    for h in range(A_HEADS):
        heads.append(_dot(o_lat[h * TQ:(h + 1) * TQ, :].astype(wuv_ref.dtype), wuv_ref[h]))
    o = jnp.concatenate(heads, axis=1)
    mix = _dot(o.astype(wout_ref.dtype), wout_ref[...])
    o_ref[0] = _layer_norm(DN_ALPHA * x_ref[0] + mix, g_ref[...], b_ref[...])


def _dsa_attn(x, qlat, qidx, widx, ckv, kidx, w_uv, w_out, ln_g, ln_b):
    bsz, seq, _ = x.shape
    nq = seq // TQ
    top_k = min(TOPK_MAX, seq // 4)
    kern = functools.partial(_dsa_attn_kernel, top_k=top_k)
    rows = A_HEADS * TQ
    return pl.pallas_call(
        kern,
        grid=(bsz, nq),
        in_specs=[
            pl.BlockSpec((1, A_HEADS, TQ, A_KV_RANK), lambda b, i: (b * nq + i, 0, 0, 0)),
            pl.BlockSpec((1, IDX_HEADS, TQ, IDX_PAD), lambda b, i: (b * nq + i, 0, 0, 0)),
            pl.BlockSpec((TQ, LANES), lambda b, i: (b * nq + i, 0)),
            pl.BlockSpec((1, seq, A_KV_RANK), lambda b, i: (b, 0, 0)),
            pl.BlockSpec((1, seq, IDX_PAD), lambda b, i: (b, 0, 0)),
            pl.BlockSpec((1, TQ, D_MODEL), lambda b, i: (b, i, 0)),
            pl.BlockSpec((A_HEADS, A_KV_RANK, A_HEAD_DIM), lambda b, i: (0, 0, 0)),
            pl.BlockSpec((D_MODEL, D_MODEL), lambda b, i: (0, 0)),
            pl.BlockSpec((1, D_MODEL), lambda b, i: (0, 0)),
            pl.BlockSpec((1, D_MODEL), lambda b, i: (0, 0)),
        ],
        out_specs=pl.BlockSpec((1, TQ, D_MODEL), lambda b, i: (b, i, 0)),
        out_shape=jax.ShapeDtypeStruct((bsz, seq, D_MODEL), jnp.float32),
        scratch_shapes=[
            pltpu.VMEM((nq, TQ, TQ), jnp.float32),
            pltpu.VMEM((nq, TQ, TQ), jnp.float32),
            pltpu.VMEM((rows, LANES), jnp.float32),
            pltpu.VMEM((rows, LANES), jnp.float32),
            pltpu.VMEM((rows, A_KV_RANK), jnp.float32),
        ],
        compiler_params=pltpu.CompilerParams(
            dimension_semantics=("arbitrary", "arbitrary"), vmem_limit_bytes=VMEM_LIMIT_BYTES),
        name="dsa_attn",
    )(qlat, qidx, widx, ckv.reshape(bsz, seq, A_KV_RANK), kidx.reshape(bsz, seq, IDX_PAD), x,
      w_uv.astype(MXU_DTYPE), w_out.astype(MXU_DTYPE), ln_g[None, :], ln_b[None, :])


def _dsa_layer(x, w_in, g_q, g_kv, w_q_lat, w_q_idx, g_kidx, b_kidx, w_uv, w_out, ln_g, ln_b):
    bsz, seq, _ = x.shape
    qlat, qidx, ckv, kidx, widx = _dsa_proj(x.reshape(bsz * seq, D_MODEL), w_in, g_q, g_kv, w_q_lat, w_q_idx,
                                            g_kidx, b_kidx)
    return _dsa_attn(x, qlat, qidx, widx, ckv, kidx, w_uv, w_out, ln_g, ln_b)


def _hgrn_kernel(x_ref, win_ref, lb_ref, go_ref, wout_ref, g_ref, b_ref, o_ref,
                 st_ref, qp_ref, kp_ref, on_ref):
    t_len = HG_T

    @pl.when(pl.program_id(1) == 0)
    def _():
        st_ref[...] = jnp.zeros_like(st_ref)

    xb = x_ref[0].astype(win_ref.dtype)
    row = lax.broadcasted_iota(jnp.int32, (t_len, B_DIM), 0)
    in_grp = row % SUBLANES
    rr = lax.broadcasted_iota(jnp.int32, (t_len, t_len), 0)
    cc = lax.broadcasted_iota(jnp.int32, (t_len, t_len), 1)
    rc_xor = rr ^ cc

    def head(h, carry):
        p = _dot(xb, win_ref[h])
        q_raw, f_raw, v, g_raw = (p[:, k * B_DIM:(k + 1) * B_DIM] for k in range(4))
        lb = lb_ref[h]
        forget = lb + (1.0 - lb) * _sigmoid(f_raw)
        log_f = jnp.log(forget)
        k = (1.0 - lb) * _sigmoid(-f_raw)
        q = q_raw * _sigmoid(q_raw)

        b = log_f
        sh = 1
        while sh < t_len:
            b = b + jnp.where(row >= sh, pltpu.roll(b, sh, axis=0), 0.0)
            sh *= 2
        b_last = b[t_len - 1:t_len, :]

        st = st_ref[h]
        o = _dot_nt((q * jnp.exp(b)).astype(xb.dtype), st.astype(xb.dtype))
        k_tail = (k * jnp.exp(b_last - b)).astype(xb.dtype)
        st_ref[h] = st * jnp.exp(b_last) + _dot_tn(v.astype(xb.dtype), k_tail)

        def grp_roll(a, j):
            return pltpu.roll(a.reshape(t_len // SUBLANES, SUBLANES, B_DIM), j, axis=1).reshape(t_len, B_DIM)

        prod = jnp.ones_like(forget)
        for j in range(SUBLANES):
            if j == 0:
                kj, vj = k, v
            else:
                kj, vj = grp_roll(k, j), grp_roll(v, j)
                prod = prod * (forget if j == 1 else grp_roll(forget, j - 1))
            term = jnp.where(in_grp >= j, q * kj * prod, 0.0)
            o = o + jnp.sum(term, axis=-1, keepdims=True) * vj

        for li, c in enumerate(HG_LEVELS):
            for m in range(c, t_len, 2 * c):
                ref_b = b[m - 1:m, :]
                qp_ref[li, m:m + c, :] = q[m:m + c, :] * jnp.exp(b[m:m + c, :] - ref_b)
                qp_ref[li, m - c:m, :] = jnp.zeros((c, B_DIM), jnp.float32)
                kp_ref[li, m - c:m, :] = k[m - c:m, :] * jnp.exp(ref_b - b[m - c:m, :])
                kp_ref[li, m:m + c, :] = jnp.zeros((c, B_DIM), jnp.float32)
        a = None
        for li in reversed(range(len(HG_LEVELS))):
            al = _dot_nt(qp_ref[li].astype(xb.dtype), kp_ref[li].astype(xb.dtype))
            a = al if a is None else jnp.where(rc_xor < 2 * HG_LEVELS[li], al, a)
        o = o + _dot(a.astype(xb.dtype), v.astype(xb.dtype))

        og = o * _sigmoid(g_raw)
        on = og * lax.rsqrt(jnp.mean(og * og, axis=-1, keepdims=True) + RMS_EPS) * go_ref[h]
        on_ref[h] = on.astype(on_ref.dtype)
        return carry

    lax.fori_loop(0, B_HEADS, head, 0, unroll=True)

    on = jnp.concatenate([on_ref[h] for h in range(B_HEADS)], axis=1)
    mix = _dot(on, wout_ref[...])
    o_ref[0] = _layer_norm(DN_ALPHA * x_ref[0] + mix, g_ref[...], b_ref[...])


def _hgrn_layer(x, w_in, lb, g_o, w_out, ln_g, ln_b):
    bsz, seq, _ = x.shape
    win = w_in.reshape(D_MODEL, 4, B_HEADS, B_DIM).transpose(2, 0, 1, 3).reshape(B_HEADS, D_MODEL, 4 * B_DIM)
    n_lv = len(HG_LEVELS)
    return pl.pallas_call(
        _hgrn_kernel,
        grid=(bsz, seq // HG_T),
        in_specs=[
            pl.BlockSpec((1, HG_T, D_MODEL), lambda b, i: (b, i, 0)),
            pl.BlockSpec((B_HEADS, D_MODEL, 4 * B_DIM), lambda b, i: (0, 0, 0)),
            pl.BlockSpec((B_HEADS, 1, B_DIM), lambda b, i: (0, 0, 0)),
            pl.BlockSpec((B_HEADS, 1, B_DIM), lambda b, i: (0, 0, 0)),
            pl.BlockSpec((D_MODEL, D_MODEL), lambda b, i: (0, 0)),
            pl.BlockSpec((1, D_MODEL), lambda b, i: (0, 0)),
            pl.BlockSpec((1, D_MODEL), lambda b, i: (0, 0)),
        ],
        out_specs=pl.BlockSpec((1, HG_T, D_MODEL), lambda b, i: (b, i, 0)),
        out_shape=jax.ShapeDtypeStruct((bsz, seq, D_MODEL), jnp.float32),
        scratch_shapes=[
            pltpu.VMEM((B_HEADS, B_DIM, B_DIM), jnp.float32),
            pltpu.VMEM((n_lv, HG_T, B_DIM), jnp.float32),
            pltpu.VMEM((n_lv, HG_T, B_DIM), jnp.float32),
            pltpu.VMEM((B_HEADS, HG_T, B_DIM), MXU_DTYPE),
        ],
        compiler_params=pltpu.CompilerParams(
            dimension_semantics=("arbitrary", "arbitrary"), vmem_limit_bytes=VMEM_LIMIT_BYTES),
        name="hgrn",
    )(x, win.astype(MXU_DTYPE), lb.reshape(B_HEADS, 1, B_DIM), g_o.reshape(B_HEADS, 1, B_DIM),
      w_out.astype(MXU_DTYPE), ln_g[None, :], ln_b[None, :])


@jax.jit
def kernel(x, a_w_in, a_g_q, a_g_kv, a_w_q_lat, a_w_q_idx, a_g_kidx, a_b_kidx, a_w_uv, a_w_out, b_w_in, b_lb_logits, b_g_o, b_w_out, ln1_g, ln1_b, f_w_up, f_conv_w, f_conv_b, f_w_down, ln2_g, ln2_b):
    bsz, seq, _ = x.shape
    c = jnp.cumsum(jax.nn.softmax(b_lb_logits.astype(jnp.float32), axis=0), axis=0)
    lower_bounds = c - c[0:1]
    for layer in range(DEPTH):
        j = layer // N_MIXERS
        if layer % N_MIXERS == 0:
            x = _dsa_layer(x, a_w_in[j], a_g_q[j], a_g_kv[j], a_w_q_lat[j], a_w_q_idx[j], a_g_kidx[j],
                           a_b_kidx[j], a_w_uv[j], a_w_out[j], ln1_g[layer], ln1_b[layer])
        else:
            x = _hgrn_layer(x, b_w_in[j], lower_bounds[layer], b_g_o[j], b_w_out[j], ln1_g[layer], ln1_b[layer])
        x = _ffn_layer(x.reshape(bsz * seq, D_MODEL), seq, f_w_up[layer], f_conv_w[layer], f_conv_b[layer],
                       f_w_down[layer], ln2_g[layer], ln2_b[layer]).reshape(bsz, seq, D_MODEL)
    return x
```

```python
import functools

import jax
import jax.numpy as jnp
from jax import lax
from jax.experimental import pallas as pl
from jax.experimental.pallas import tpu as pltpu

D_MODEL = 1024
DEPTH = 4
N_MIXERS = 2
A_HEADS = 8
A_HEAD_DIM = D_MODEL // A_HEADS
A_Q_RANK = 384
A_KV_RANK = 256
IDX_HEADS = 8
IDX_DIM = 64
TOPK_MAX = 256
B_HEADS = 8
B_DIM = D_MODEL // B_HEADS
D_FF = 2816
DN_ALPHA = (2 * DEPTH) ** 0.25
LN_EPS = 1e-5
RMS_EPS = 1e-6
LOG2_E = 1.4426950408889634

LANES = 128
SUBLANES = 8
VMEM_LIMIT_BYTES = 56 * 1024 * 1024

MXU_DTYPE = jnp.bfloat16

IDX_PAD = LANES
TQ = 256
ATTN_SPAN = 2
HG_T = 256
HG_LEVELS = (8, 16, 32, 64, 128)
FFN_TM = 1024
FFN_HALO = SUBLANES
FFN_SLAB = 256

SEARCH_STEPS = 31
SEARCH_FIXED_STEPS = 24
COUNT_ROWS = 4 * SUBLANES
SCORE_MASKED = -3.0e38
SCORE_ABOVE_MASKED = -2.9e38
BIAS_MASKED = -1.0e30


def _dot(a, b):
    return jnp.dot(a, b, preferred_element_type=jnp.float32)


def _dot_nt(a, b):
    return lax.dot_general(a, b, (((1,), (1,)), ((), ())), preferred_element_type=jnp.float32)


def _dot_tn(a, b):
    return lax.dot_general(a, b, (((0,), (0,)), ((), ())), preferred_element_type=jnp.float32)


def _layer_norm(v, g, b):
    mu = jnp.mean(v, axis=-1, keepdims=True)
    c = v - mu
    var = jnp.mean(c * c, axis=-1, keepdims=True)
    return c * lax.rsqrt(var + LN_EPS) * g + b


def _sigmoid(v):
    return 1.0 / (1.0 + jnp.exp(-v))


def _ffn_kernel(xh_ref, x_ref, wa_ref, wu_ref, cwa_ref, cwu_ref, cba_ref, cbu_ref, wd_ref, g_ref, b_ref,
                o_ref, hid_ref, *, tiles_per_seq):
    i = pl.program_id(0)
    first = (i % tiles_per_seq) == 0
    halo = jnp.where(first, 0.0, xh_ref[...])
    xb = jnp.concatenate([halo, x_ref[...]], axis=0).astype(wa_ref.dtype)

    def conv(h, cw, cb):
        h1 = pltpu.roll(h, 1, axis=0)
        h2 = pltpu.roll(h, 2, axis=0)
        y = h * cw[2:3, :] + h1 * cw[1:2, :] + h2 * cw[0:1, :] + cb
        return y[FFN_HALO:, :]

    for c in range(D_FF // FFN_SLAB):
        cs = slice(c * FFN_SLAB, (c + 1) * FFN_SLAB)
        a = conv(_dot(xb, wa_ref[:, cs]), cwa_ref[:, cs], cba_ref[:, cs])
        u = conv(_dot(xb, wu_ref[:, cs]), cwu_ref[:, cs], cbu_ref[:, cs])
        hid_ref[:, cs] = ((a * _sigmoid(a)) * u).astype(hid_ref.dtype)
    mix = _dot(hid_ref[...], wd_ref[...])
    o_ref[...] = _layer_norm(DN_ALPHA * x_ref[...] + mix, g_ref[...], b_ref[...])


def _ffn_layer(x2d, seq, w_up, conv_w, conv_b, w_down, ln_g, ln_b):
    n = x2d.shape[0]
    tm = FFN_TM
    wa = w_up[:, :D_FF].astype(MXU_DTYPE)
    wu = w_up[:, D_FF:].astype(MXU_DTYPE)
    cw = conv_w[:, 0, :]
    cwa, cwu = cw[:, :D_FF], cw[:, D_FF:]
    cba, cbu = conv_b[None, :D_FF], conv_b[None, D_FF:]
    wd = w_down.astype(MXU_DTYPE)
    halo_blocks = tm // FFN_HALO
    kern = functools.partial(_ffn_kernel, tiles_per_seq=seq // tm)
    whole = lambda shape: pl.BlockSpec(shape, lambda i: (0,) * len(shape), pipeline_mode=pl.Buffered(1))
    return pl.pallas_call(
        kern,
        grid=(n // tm,),
        in_specs=[
            pl.BlockSpec((FFN_HALO, D_MODEL), lambda i: (jnp.maximum(i * halo_blocks - 1, 0), 0)),
            pl.BlockSpec((tm, D_MODEL), lambda i: (i, 0)),
            whole((D_MODEL, D_FF)), whole((D_MODEL, D_FF)),
            whole((3, D_FF)), whole((3, D_FF)), whole((1, D_FF)), whole((1, D_FF)),
            whole((D_FF, D_MODEL)), whole((1, D_MODEL)), whole((1, D_MODEL)),
        ],
        out_specs=pl.BlockSpec((tm, D_MODEL), lambda i: (i, 0)),
        out_shape=jax.ShapeDtypeStruct((n, D_MODEL), jnp.float32),
        scratch_shapes=[pltpu.VMEM((tm, D_FF), MXU_DTYPE)],
        compiler_params=pltpu.CompilerParams(
            dimension_semantics=("arbitrary",), vmem_limit_bytes=VMEM_LIMIT_BYTES),
        name="ffn",
    )(x2d, x2d, wa, wu, cwa, cwu, cba, cbu, wd, ln_g[None, :], ln_b[None, :])


_PQ0, _PQ1 = 0, A_Q_RANK
_PKV0, _PKV1 = _PQ1, _PQ1 + A_KV_RANK
_PKI0, _PKI1 = _PKV1, _PKV1 + LANES
_PWI0, _PWI1 = _PKI1, _PKI1 + LANES


def _dsa_proj_kernel(x_ref, wcat_ref, gq_ref, gkv_ref, gk_ref, bk_ref, wql_ref, wqi_ref,
                     qlat_ref, qidx_ref, ckv_ref, kidx_ref, widx_ref):
    xb = x_ref[...].astype(wcat_ref.dtype)
    proj = _dot(xb, wcat_ref[...])
    pq = proj[:, _PQ0:_PQ1]
    cq = pq * lax.rsqrt(jnp.mean(pq * pq, axis=-1, keepdims=True) + RMS_EPS) * gq_ref[...]
    pkv = proj[:, _PKV0:_PKV1]
    ckv = pkv * lax.rsqrt(jnp.mean(pkv * pkv, axis=-1, keepdims=True) + RMS_EPS) * gkv_ref[...]
    ckv_ref[...] = ckv.astype(ckv_ref.dtype)
    pk = proj[:, _PKI0:_PKI1]
    real = lax.broadcasted_iota(jnp.int32, pk.shape, 1) < IDX_DIM
    mu = jnp.sum(pk, axis=-1, keepdims=True) * (1.0 / IDX_DIM)
    c = jnp.where(real, pk - mu, 0.0)
    var = jnp.sum(c * c, axis=-1, keepdims=True) * (1.0 / IDX_DIM)
    kidx = c * lax.rsqrt(var + LN_EPS) * gk_ref[...] + bk_ref[...]
    kidx_ref[...] = kidx.astype(kidx_ref.dtype)
    widx_ref[...] = proj[:, _PWI0:_PWI1] * (IDX_HEADS ** -0.5 * IDX_DIM ** -0.5)
    cqb = cq.astype(wql_ref.dtype)
    ql = _dot(cqb, wql_ref[...]) * (A_KV_RANK ** -0.5 * LOG2_E)
    qi = _dot(cqb, wqi_ref[...])
    for h in range(A_HEADS):
        qlat_ref[0, h] = ql[:, h * A_KV_RANK:(h + 1) * A_KV_RANK].astype(qlat_ref.dtype)
    for h in range(IDX_HEADS):
        qidx_ref[0, h] = qi[:, h * IDX_PAD:(h + 1) * IDX_PAD].astype(qidx_ref.dtype)


def _dsa_proj(x2d, w_in, g_q, g_kv, w_q_lat, w_q_idx, g_kidx, b_kidx):
    n = x2d.shape[0]
    nblk = n // TQ
    o1, o2, o3 = A_Q_RANK, A_Q_RANK + A_KV_RANK, A_Q_RANK + A_KV_RANK + IDX_DIM
    zpad = lambda w, cols: jnp.pad(w, ((0, 0), (0, cols - w.shape[1])))
    wcat = jnp.concatenate(
        [w_in[:, :o2], zpad(w_in[:, o2:o3], LANES), zpad(w_in[:, o3:], LANES)], axis=1).astype(MXU_DTYPE)
    pcols = wcat.shape[1]
    gk = zpad(g_kidx[None, :], LANES)
    bk = zpad(b_kidx[None, :], LANES)
    wqi = jnp.pad(w_q_idx.reshape(A_Q_RANK, IDX_HEADS, IDX_DIM), ((0, 0), (0, 0), (0, IDX_PAD - IDX_DIM)))
    wqi = wqi.reshape(A_Q_RANK, IDX_HEADS * IDX_PAD).astype(MXU_DTYPE)
    wql = w_q_lat.astype(MXU_DTYPE)
    full = lambda shape: pl.BlockSpec(shape, lambda i: (0,) * len(shape))
    return pl.pallas_call(
        _dsa_proj_kernel,
        grid=(nblk,),
        in_specs=[
            pl.BlockSpec((TQ, D_MODEL), lambda i: (i, 0)),
            full((D_MODEL, pcols)), full((1, A_Q_RANK)), full((1, A_KV_RANK)), full((1, LANES)), full((1, LANES)),
            full((A_Q_RANK, A_HEADS * A_KV_RANK)), full((A_Q_RANK, IDX_HEADS * IDX_PAD)),
        ],
        out_specs=[
            pl.BlockSpec((1, A_HEADS, TQ, A_KV_RANK), lambda i: (i, 0, 0, 0)),
            pl.BlockSpec((1, IDX_HEADS, TQ, IDX_PAD), lambda i: (i, 0, 0, 0)),
            pl.BlockSpec((TQ, A_KV_RANK), lambda i: (i, 0)),
            pl.BlockSpec((TQ, IDX_PAD), lambda i: (i, 0)),
            pl.BlockSpec((TQ, LANES), lambda i: (i, 0)),
        ],
        out_shape=[
            jax.ShapeDtypeStruct((nblk, A_HEADS, TQ, A_KV_RANK), MXU_DTYPE),
            jax.ShapeDtypeStruct((nblk, IDX_HEADS, TQ, IDX_PAD), MXU_DTYPE),
            jax.ShapeDtypeStruct((n, A_KV_RANK), MXU_DTYPE),
            jax.ShapeDtypeStruct((n, IDX_PAD), MXU_DTYPE),
            jax.ShapeDtypeStruct((n, LANES), jnp.float32),
        ],
        compiler_params=pltpu.CompilerParams(
            dimension_semantics=("arbitrary",), vmem_limit_bytes=VMEM_LIMIT_BYTES),
        name="dsa_proj",
    )(x2d, wcat, g_q[None, :], g_kv[None, :], gk, bk, wql, wqi)


def _key_to_f32(key):
    bits = key ^ ((key >> 31) & jnp.int32(0x7FFFFFFF))
    return lax.bitcast_convert_type(bits, jnp.float32)


def _dsa_attn_kernel(qlat_ref, qidx_ref, widx_ref, ckv_ref, kidx_ref, x_ref, wuv_ref, wout_ref, g_ref, b_ref,
                     o_ref, sc_ref, sct_ref, m_ref, l_ref, acc_ref, *, top_k):
    i = pl.program_id(1)
    n_tiles = i + 1
    rows = A_HEADS * TQ
    kf = jnp.float32(top_k)
    row_id = lax.broadcasted_iota(jnp.int32, (TQ, TQ), 0)
    col_id = lax.broadcasted_iota(jnp.int32, (TQ, TQ), 1)

    qi = qidx_ref[0].reshape(IDX_HEADS * TQ, IDX_PAD)
    w = widx_ref[...]

    def score_tiles(jj, carry):
        for u in range(ATTN_SPAN):
            j = ATTN_SPAN * jj + u
            kt = kidx_ref[0, pl.ds(pl.multiple_of(j * TQ, TQ), TQ), :]
            z = _dot_nt(qi, kt)
            s = jnp.zeros((TQ, TQ), jnp.float32)
            for h in range(IDX_HEADS):
                s = s + w[:, h:h + 1] * jnp.maximum(z[h * TQ:(h + 1) * TQ, :], 0.0)
            valid = (j * TQ + col_id) <= (i * TQ + row_id)
            s = jnp.where(valid, s, SCORE_MASKED)
            sc_ref[j] = s
            sct_ref[j] = s.T
        return carry

    lax.fori_loop(0, (i + ATTN_SPAN) // ATTN_SPAN, score_tiles, 0)

    def count(pred):
        def body(j, acc):
            hit = jnp.where(pred(sct_ref[j]), 1.0, 0.0)
            return acc + jnp.sum(hit.reshape(TQ // COUNT_ROWS, COUNT_ROWS, TQ), axis=0)
        acc = lax.fori_loop(0, n_tiles, body, jnp.zeros((COUNT_ROWS, TQ), jnp.float32))
        return jnp.sum(acc, axis=0, keepdims=True)

    n_nonneg = count(lambda s: s >= 0.0)
    nonneg = n_nonneg >= kf
    key0 = jnp.where(nonneg, jnp.int32(0), jnp.int32(-2 ** 31))
    n_ge0 = jnp.where(nonneg, n_nonneg, (n_tiles * TQ).astype(jnp.float32))
    short = (i * TQ + 1 + lax.broadcasted_iota(jnp.int32, (1, TQ), 1)) < top_k

    def bit_step(it, state):
        key, n_key = state
        cand = key | lax.shift_left(jnp.int32(1), 30 - it)
        cand_f = _key_to_f32(cand)
        n_ge = count(lambda s: s >= cand_f)
        accept = n_ge >= kf
        return jnp.where(accept, cand, key), jnp.where(accept, n_ge, n_key)

    def unsettled(n_key):
        return jnp.max(jnp.where((n_key == kf) | short, 0, 1))

    def late_step(state):
        it, _, key, n_key = state
        key, n_key = bit_step(it, (key, n_key))
        return it + 1, unsettled(n_key), key, n_key

    key, n_key = lax.fori_loop(0, SEARCH_FIXED_STEPS, bit_step, (key0, n_ge0))
    _, still_unsettled, key, _ = lax.while_loop(
        lambda st: (st[1] > 0) & (st[0] < SEARCH_STEPS), late_step,
        (jnp.int32(SEARCH_FIXED_STEPS), unsettled(n_key), key, n_key))
    thr = jnp.where(short, SCORE_MASKED, _key_to_f32(key))

    def to_tile(v):
        cols = jnp.broadcast_to(v, (LANES, TQ)).T
        return jnp.concatenate([cols] * (TQ // LANES), axis=1)

    def bias_settled():
        floor = to_tile(jnp.where(short, SCORE_ABOVE_MASKED, thr))

        def tile(j, carry):
            sc_ref[j] = jnp.where(sc_ref[j] >= floor, 0.0, BIAS_MASKED)
            return carry

        lax.fori_loop(0, n_tiles, tile, 0)
        return 0

    def bias_with_ties():
        need2 = to_tile(jnp.where(short, 0.0, kf - count(lambda s: s > thr)))
        thr2 = to_tile(thr)
        tri = (row_id <= col_id).astype(wuv_ref.dtype)

        def tile(j, seen):
            s = sc_ref[j]
            eq = s == thr2
            prefix = _dot(jnp.where(eq, 1.0, 0.0).astype(tri.dtype), tri) + seen
            take = (s > thr2) | (eq & (prefix <= need2))
            sc_ref[j] = jnp.where(take, 0.0, BIAS_MASKED)
            return prefix[:, TQ - 1:TQ]

        lax.fori_loop(0, n_tiles, tile, jnp.zeros((TQ, 1), jnp.float32))
        return 0

    lax.cond(still_unsettled == 0, bias_settled, bias_with_ties)

    @pl.when(i % ATTN_SPAN == 0)
    def _():
        sc_ref[i + 1] = jnp.full((TQ, TQ), BIAS_MASKED, jnp.float32)

    q = qlat_ref[0].reshape(rows, A_KV_RANK)
    m_ref[...] = jnp.full(m_ref.shape, BIAS_MASKED, jnp.float32)
    l_ref[...] = jnp.zeros_like(l_ref)
    acc_ref[...] = jnp.zeros_like(acc_ref)
    tk = ATTN_SPAN * TQ

    def lanes(v, width):
        return jnp.concatenate([v] * (width // LANES), axis=1)

    def attn_tile(jj, carry):
        kv = ckv_ref[0, pl.ds(pl.multiple_of(jj * tk, tk), tk), :]
        bias = jnp.concatenate([sc_ref[ATTN_SPAN * jj + u] for u in range(ATTN_SPAN)], axis=1)
        for h in range(A_HEADS):
            hs = slice(h * TQ, (h + 1) * TQ)
            logits = _dot_nt(q[hs], kv) + bias
            m_old = m_ref[hs, :]
            m_new = jnp.maximum(m_old, jnp.max(logits, axis=1, keepdims=True))
            p = jnp.exp2(logits - lanes(m_new, tk))
            alpha = jnp.exp2(m_old - m_new)
            p_lanes = sum(p[:, u * LANES:(u + 1) * LANES] for u in range(tk // LANES))
            l_ref[hs, :] = alpha * l_ref[hs, :] + p_lanes
            m_ref[hs, :] = m_new
            acc_ref[hs, :] = lanes(alpha, A_KV_RANK) * acc_ref[hs, :] + _dot(p.astype(kv.dtype), kv)
        return carry

    lax.fori_loop(0, (i + ATTN_SPAN) // ATTN_SPAN, attn_tile, 0)

    o_lat = acc_ref[...] / jnp.sum(l_ref[...], axis=1, keepdims=True)
    heads = []
    for h in range(A_HEADS):
        heads.append(_dot(o_lat[h * TQ:(h + 1) * TQ, :].astype(wuv_ref.dtype), wuv_ref[h]))
    o = jnp.concatenate(heads, axis=1)
    mix = _dot(o.astype(wout_ref.dtype), wout_ref[...])
    o_ref[0] = _layer_norm(DN_ALPHA * x_ref[0] + mix, g_ref[...], b_ref[...])


def _dsa_attn(x, qlat, qidx, widx, ckv, kidx, w_uv, w_out, ln_g, ln_b):
    bsz, seq, _ = x.shape
    nq = seq // TQ
    top_k = min(TOPK_MAX, seq // 4)
    kern = functools.partial(_dsa_attn_kernel, top_k=top_k)
    rows = A_HEADS * TQ
    return pl.pallas_call(
        kern,
        grid=(bsz, nq),
        in_specs=[
            pl.BlockSpec((1, A_HEADS, TQ, A_KV_RANK), lambda b, i: (b * nq + i, 0, 0, 0)),
            pl.BlockSpec((1, IDX_HEADS, TQ, IDX_PAD), lambda b, i: (b * nq + i, 0, 0, 0)),
            pl.BlockSpec((TQ, LANES), lambda b, i: (b * nq + i, 0)),
            pl.BlockSpec((1, seq, A_KV_RANK), lambda b, i: (b, 0, 0)),
            pl.BlockSpec((1, seq, IDX_PAD), lambda b, i: (b, 0, 0)),
            pl.BlockSpec((1, TQ, D_MODEL), lambda b, i: (b, i, 0)),
            pl.BlockSpec((A_HEADS, A_KV_RANK, A_HEAD_DIM), lambda b, i: (0, 0, 0)),
            pl.BlockSpec((D_MODEL, D_MODEL), lambda b, i: (0, 0)),
            pl.BlockSpec((1, D_MODEL), lambda b, i: (0, 0)),
            pl.BlockSpec((1, D_MODEL), lambda b, i: (0, 0)),
        ],
        out_specs=pl.BlockSpec((1, TQ, D_MODEL), lambda b, i: (b, i, 0)),
        out_shape=jax.ShapeDtypeStruct((bsz, seq, D_MODEL), jnp.float32),
        scratch_shapes=[
            pltpu.VMEM((nq, TQ, TQ), jnp.float32),
            pltpu.VMEM((nq, TQ, TQ), jnp.float32),
            pltpu.VMEM((rows, LANES), jnp.float32),
            pltpu.VMEM((rows, LANES), jnp.float32),
            pltpu.VMEM((rows, A_KV_RANK), jnp.float32),
        ],
        compiler_params=pltpu.CompilerParams(
            dimension_semantics=("arbitrary", "arbitrary"), vmem_limit_bytes=VMEM_LIMIT_BYTES),
        name="dsa_attn",
    )(qlat, qidx, widx, ckv.reshape(bsz, seq, A_KV_RANK), kidx.reshape(bsz, seq, IDX_PAD), x,
      w_uv.astype(MXU_DTYPE), w_out.astype(MXU_DTYPE), ln_g[None, :], ln_b[None, :])


def _dsa_layer(x, w_in, g_q, g_kv, w_q_lat, w_q_idx, g_kidx, b_kidx, w_uv, w_out, ln_g, ln_b):
    bsz, seq, _ = x.shape
    qlat, qidx, ckv, kidx, widx = _dsa_proj(x.reshape(bsz * seq, D_MODEL), w_in, g_q, g_kv, w_q_lat, w_q_idx,
                                            g_kidx, b_kidx)
    return _dsa_attn(x, qlat, qidx, widx, ckv, kidx, w_uv, w_out, ln_g, ln_b)


def _hgrn_kernel(x_ref, win_ref, lb_ref, go_ref, wout_ref, g_ref, b_ref, o_ref,
                 st_ref, qp_ref, kp_ref, on_ref):
    t_len = HG_T

    @pl.when(pl.program_id(1) == 0)
    def _():
        st_ref[...] = jnp.zeros_like(st_ref)

    xb = x_ref[0].astype(win_ref.dtype)
    row = lax.broadcasted_iota(jnp.int32, (t_len, B_DIM), 0)
    in_grp = row % SUBLANES
    rr = lax.broadcasted_iota(jnp.int32, (t_len, t_len), 0)
    cc = lax.broadcasted_iota(jnp.int32, (t_len, t_len), 1)
    rc_xor = rr ^ cc

    def head(h, carry):
        p = _dot(xb, win_ref[h])
        q_raw, f_raw, v, g_raw = (p[:, k * B_DIM:(k + 1) * B_DIM] for k in range(4))
        lb = lb_ref[h]
        forget = lb + (1.0 - lb) * _sigmoid(f_raw)
        log_f = jnp.log(forget)
        k = (1.0 - lb) * _sigmoid(-f_raw)
        q = q_raw * _sigmoid(q_raw)

        b = log_f
        sh = 1
        while sh < t_len:
            b = b + jnp.where(row >= sh, pltpu.roll(b, sh, axis=0), 0.0)
            sh *= 2
        b_last = b[t_len - 1:t_len, :]

        st = st_ref[h]
        o = _dot_nt((q * jnp.exp(b)).astype(xb.dtype), st.astype(xb.dtype))
        k_tail = (k * jnp.exp(b_last - b)).astype(xb.dtype)
        st_ref[h] = st * jnp.exp(b_last) + _dot_tn(v.astype(xb.dtype), k_tail)

        def grp_roll(a, j):
            return pltpu.roll(a.reshape(t_len // SUBLANES, SUBLANES, B_DIM), j, axis=1).reshape(t_len, B_DIM)

        prod = jnp.ones_like(forget)
        for j in range(SUBLANES):
            if j == 0:
                kj, vj = k, v
            else:
                kj, vj = grp_roll(k, j), grp_roll(v, j)
                prod = prod * (forget if j == 1 else grp_roll(forget, j - 1))
            term = jnp.where(in_grp >= j, q * kj * prod, 0.0)
            o = o + jnp.sum(term, axis=-1, keepdims=True) * vj

        for li, c in enumerate(HG_LEVELS):
            for m in range(c, t_len, 2 * c):
                ref_b = b[m - 1:m, :]
                qp_ref[li, m:m + c, :] = q[m:m + c, :] * jnp.exp(b[m:m + c, :] - ref_b)
                qp_ref[li, m - c:m, :] = jnp.zeros((c, B_DIM), jnp.float32)
                kp_ref[li, m - c:m, :] = k[m - c:m, :] * jnp.exp(ref_b - b[m - c:m, :])
                kp_ref[li, m:m + c, :] = jnp.zeros((c, B_DIM), jnp.float32)
        a = None
        for li in reversed(range(len(HG_LEVELS))):
            al = _dot_nt(qp_ref[li].astype(xb.dtype), kp_ref[li].astype(xb.dtype))
            a = al if a is None else jnp.where(rc_xor < 2 * HG_LEVELS[li], al, a)
        o = o + _dot(a.astype(xb.dtype), v.astype(xb.dtype))

        og = o * _sigmoid(g_raw)
        on = og * lax.rsqrt(jnp.mean(og * og, axis=-1, keepdims=True) + RMS_EPS) * go_ref[h]
        on_ref[h] = on.astype(on_ref.dtype)
        return carry

    lax.fori_loop(0, B_HEADS, head, 0, unroll=True)

    on = jnp.concatenate([on_ref[h] for h in range(B_HEADS)], axis=1)
    mix = _dot(on, wout_ref[...])
    o_ref[0] = _layer_norm(DN_ALPHA * x_ref[0] + mix, g_ref[...], b_ref[...])


def _hgrn_layer(x, w_in, lb, g_o, w_out, ln_g, ln_b):
    bsz, seq, _ = x.shape
    win = w_in.reshape(D_MODEL, 4, B_HEADS, B_DIM).transpose(2, 0, 1, 3).reshape(B_HEADS, D_MODEL, 4 * B_DIM)
    n_lv = len(HG_LEVELS)
    return pl.pallas_call(
        _hgrn_kernel,
        grid=(bsz, seq // HG_T),
        in_specs=[
            pl.BlockSpec((1, HG_T, D_MODEL), lambda b, i: (b, i, 0)),
            pl.BlockSpec((B_HEADS, D_MODEL, 4 * B_DIM), lambda b, i: (0, 0, 0)),
            pl.BlockSpec((B_HEADS, 1, B_DIM), lambda b, i: (0, 0, 0)),
            pl.BlockSpec((B_HEADS, 1, B_DIM), lambda b, i: (0, 0, 0)),
            pl.BlockSpec((D_MODEL, D_MODEL), lambda b, i: (0, 0)),
            pl.BlockSpec((1, D_MODEL), lambda b, i: (0, 0)),
            pl.BlockSpec((1, D_MODEL), lambda b, i: (0, 0)),
        ],
        out_specs=pl.BlockSpec((1, HG_T, D_MODEL), lambda b, i: (b, i, 0)),
        out_shape=jax.ShapeDtypeStruct((bsz, seq, D_MODEL), jnp.float32),
        scratch_shapes=[
            pltpu.VMEM((B_HEADS, B_DIM, B_DIM), jnp.float32),
            pltpu.VMEM((n_lv, HG_T, B_DIM), jnp.float32),
            pltpu.VMEM((n_lv, HG_T, B_DIM), jnp.float32),
            pltpu.VMEM((B_HEADS, HG_T, B_DIM), MXU_DTYPE),
        ],
        compiler_params=pltpu.CompilerParams(
            dimension_semantics=("arbitrary", "arbitrary"), vmem_limit_bytes=VMEM_LIMIT_BYTES),
        name="hgrn",
    )(x, win.astype(MXU_DTYPE), lb.reshape(B_HEADS, 1, B_DIM), g_o.reshape(B_HEADS, 1, B_DIM),
      w_out.astype(MXU_DTYPE), ln_g[None, :], ln_b[None, :])


@jax.jit
def kernel(x, a_w_in, a_g_q, a_g_kv, a_w_q_lat, a_w_q_idx, a_g_kidx, a_b_kidx, a_w_uv, a_w_out, b_w_in, b_lb_logits, b_g_o, b_w_out, ln1_g, ln1_b, f_w_up, f_conv_w, f_conv_b, f_w_down, ln2_g, ln2_b):
    bsz, seq, _ = x.shape
    c = jnp.cumsum(jax.nn.softmax(b_lb_logits.astype(jnp.float32), axis=0), axis=0)
    lower_bounds = c - c[0:1]
    for layer in range(DEPTH):
        j = layer // N_MIXERS
        if layer % N_MIXERS == 0:
            x = _dsa_layer(x, a_w_in[j], a_g_q[j], a_g_kv[j], a_w_q_lat[j], a_w_q_idx[j], a_g_kidx[j],
                           a_b_kidx[j], a_w_uv[j], a_w_out[j], ln1_g[layer], ln1_b[layer])
        else:
            x = _hgrn_layer(x, b_w_in[j], lower_bounds[layer], b_g_o[j], b_w_out[j], ln1_g[layer], ln1_b[layer])
        x = _ffn_layer(x.reshape(bsz * seq, D_MODEL), seq, f_w_up[layer], f_conv_w[layer], f_conv_b[layer],
                       f_w_down[layer], ln2_g[layer], ln2_b[layer]).reshape(bsz, seq, D_MODEL)
    return x
```
